```python
import jax, jax.numpy as jnp
from jax import lax
import numpy as np

D_MODEL = 1024
BATCH = 8
SEQ = 8192
DEPTH = 1
DEC_BATCH = 16
DEC_SEQ = 32
PAST_LEN = 2048

CHUNK = 64
N_HEADS = 8
HEAD_DIM = 128
N_KV_HEADS = 2
ATTN_DIM = N_HEADS * HEAD_DIM
KV_DIM = N_KV_HEADS * HEAD_DIM
IDX_HEADS = 8
IDX_DIM = 64
TOPK_MAX = 256
CONV_DIM = D_MODEL
CONV_WIDTH = 31
Q_BLOCK = 128
EPS = 1e-6
IN_SIZES = (ATTN_DIM, KV_DIM, KV_DIM, ATTN_DIM, IDX_HEADS * IDX_DIM, IDX_HEADS, IDX_DIM,
            2 * CONV_DIM, CONV_DIM, 2 * D_MODEL)
N_IN = 2 * ATTN_DIM + 2 * KV_DIM + IDX_HEADS * IDX_DIM + IDX_HEADS + IDX_DIM + 3 * CONV_DIM + 2 * D_MODEL

kernel_name = "dsa_conformer_griffin_stream_step"


def rms_norm(x, g):
    xf = x.astype(jnp.float32)
    y = xf * lax.rsqrt(jnp.mean(xf * xf, axis=-1, keepdims=True) + EPS)
    return (y * g.astype(jnp.float32)).astype(x.dtype)


def layer_norm(x, g, b):
    xf = x.astype(jnp.float32)
    mu = jnp.mean(xf, axis=-1, keepdims=True)
    var = jnp.mean(jnp.square(xf - mu), axis=-1, keepdims=True)
    y = (xf - mu) * lax.rsqrt(var + EPS) * g.astype(jnp.float32) + b.astype(jnp.float32)
    return y.astype(x.dtype)


def split_in(z):
    offsets = np.cumsum(IN_SIZES)[:-1].tolist()
    return jnp.split(z, offsets, axis=-1)


def sparse_attend(q, iq, iw, qpos, k, v, ik, topk):
    B, Q = q.shape[0], q.shape[1]
    L = k.shape[1]
    kpos = jnp.arange(L)
    qchunk = qpos // CHUNK
    adm = (kpos // CHUNK)[None, :] <= qchunk[:, None]
    dots = jnp.einsum('bqhd,bsd->bqhs', iq.astype(jnp.float32), ik.astype(jnp.float32)) * (IDX_DIM ** -0.5)
    w = iw.astype(jnp.float32) * (IDX_HEADS ** -0.5)
    score = jnp.einsum('bqh,bqhs->bqs', w, jax.nn.relu(dots))
    score = jnp.where(adm[None], score, -jnp.inf)
    _, idx = lax.top_k(score, topk)
    valid = (idx // CHUNK) <= qchunk[None, :, None]
    gather = jax.vmap(lambda rows, ids: rows[ids])
    ksel = gather(k, idx)
    vsel = gather(v, idx)
    qg = q.reshape(B, Q, N_KV_HEADS, N_HEADS // N_KV_HEADS, HEAD_DIM)
    s = jnp.einsum('bqgrd,bqkgd->bqgrk', qg.astype(jnp.float32), ksel.astype(jnp.float32)) * (HEAD_DIM ** -0.5)
    s = jnp.where(valid[:, :, None, None, :], s, -jnp.inf)
    p = jax.nn.softmax(s, axis=-1)
    o = jnp.einsum('bqgrk,bqkgd->bqgrd', p.astype(v.dtype), vsel)
    return o.reshape(B, Q, ATTN_DIM)


def attend_blocks(q, iq, iw, qpos, k, v, ik, topk):
    B, T = q.shape[0], q.shape[1]
    if T > Q_BLOCK and T % Q_BLOCK == 0:
        nb = T // Q_BLOCK

        def to_blocks(a):
            return jnp.moveaxis(a.reshape(a.shape[0], nb, Q_BLOCK, *a.shape[2:]), 1, 0)

        xs = (to_blocks(q), to_blocks(iq), to_blocks(iw), qpos.reshape(nb, Q_BLOCK))
        out = lax.map(lambda blk: sparse_attend(blk[0], blk[1], blk[2], blk[3], k, v, ik, topk), xs)
        return jnp.moveaxis(out, 0, 1).reshape(B, T, ATTN_DIM)
    return sparse_attend(q, iq, iw, qpos, k, v, ik, topk)


def mixer_layer(x, c, past_k, past_v, past_ik, conv_state,
                w_ada, b_ada, g_norm, w_in, w_a, w_dw, b_dw, g_ln, b_ln, w_b, w_out):
    B, T, _ = x.shape
    P = past_k.shape[1]
    mod = jax.nn.silu(c) @ w_ada + b_ada
    shift, scale, gate = jnp.split(mod, 3, axis=-1)
    h = rms_norm(x, g_norm) * (1 + scale[:, None, :]) + shift[:, None, :]
    z = h @ w_in
    q, k, v, gate_a, iq, iw, ik, glu, gate_b, merge = split_in(z)
    q = q.reshape(B, T, N_HEADS, HEAD_DIM)
    k = k.reshape(B, T, N_KV_HEADS, HEAD_DIM)
    v = v.reshape(B, T, N_KV_HEADS, HEAD_DIM)
    iq = iq.reshape(B, T, IDX_HEADS, IDX_DIM)
    k_all = jnp.concatenate([past_k, k], axis=1)
    v_all = jnp.concatenate([past_v, v], axis=1)
    ik_all = jnp.concatenate([past_ik, ik], axis=1)
    L = P + T
    topk = min(TOPK_MAX, L // 4)
    qpos = P + jnp.arange(T)
    o_a = attend_blocks(q, iq, iw, qpos, k_all, v_all, ik_all, topk)
    y_a = (o_a * jax.nn.silu(gate_a)) @ w_a
    ga, gb = jnp.split(glu, 2, axis=-1)
    u = ga * jax.nn.sigmoid(gb)
    u_ext = jnp.concatenate([conv_state, u], axis=1)
    dw = lax.conv_general_dilated(u_ext, w_dw[:, None, :], (1,), 'VALID',
                                  dimension_numbers=('NWC', 'WIO', 'NWC'),
                                  feature_group_count=CONV_DIM) + b_dw
    n = layer_norm(dw, g_ln, b_ln)
    y_b = (jax.nn.silu(n) * jax.nn.silu(gate_b)) @ w_b
    m_a, m_b = jnp.split(merge, 2, axis=-1)
    mixed = jax.nn.sigmoid(m_a) * y_a + jax.nn.sigmoid(m_b) * y_b
    x = x + gate[:, None, :] * (mixed @ w_out)
    new_conv = u_ext[:, -(CONV_WIDTH - 1):]
    return x, k, v, ik, new_conv


def setup_inputs(seed: int = 0) -> dict:
    key = jax.random.key(seed)
    ks = jax.random.split(key, 20)
    nrm = jax.random.normal
    f32 = jnp.float32
    sd = D_MODEL ** -0.5
    return {
        "x_prompt": nrm(ks[0], (BATCH, SEQ, D_MODEL), f32),
        "x_sample": nrm(ks[1], (DEC_BATCH, DEC_SEQ, D_MODEL), f32),
        "cache_k": nrm(ks[2], (DEPTH, DEC_BATCH, PAST_LEN, N_KV_HEADS, HEAD_DIM), f32),
        "cache_v": nrm(ks[3], (DEPTH, DEC_BATCH, PAST_LEN, N_KV_HEADS, HEAD_DIM), f32),
        "cache_idx_k": nrm(ks[4], (DEPTH, DEC_BATCH, PAST_LEN, IDX_DIM), f32),
        "state_conv": 0.5 * nrm(ks[5], (DEPTH, DEC_BATCH, CONV_WIDTH - 1, CONV_DIM), f32),
        "c_prompt": nrm(ks[6], (BATCH, D_MODEL), f32),
        "c_sample": nrm(ks[7], (DEC_BATCH, D_MODEL), f32),
        "w_ada": 0.1 * sd * nrm(ks[8], (DEPTH, D_MODEL, 3 * D_MODEL), f32),
        "b_ada": 0.01 * nrm(ks[9], (DEPTH, 3 * D_MODEL), f32),
        "g_norm": 1.0 + 0.01 * nrm(ks[10], (DEPTH, D_MODEL), f32),
        "w_in": sd * nrm(ks[11], (DEPTH, D_MODEL, N_IN), f32),
        "w_a": (ATTN_DIM ** -0.5) * nrm(ks[12], (DEPTH, ATTN_DIM, D_MODEL), f32),
        "w_dw": (CONV_WIDTH ** -0.5) * nrm(ks[13], (DEPTH, CONV_WIDTH, CONV_DIM), f32),
        "b_dw": 0.01 * nrm(ks[14], (DEPTH, CONV_DIM), f32),
        "g_ln": 1.0 + 0.01 * nrm(ks[15], (DEPTH, CONV_DIM), f32),
        "b_ln": 0.01 * nrm(ks[16], (DEPTH, CONV_DIM), f32),
        "w_b": (CONV_DIM ** -0.5) * nrm(ks[17], (DEPTH, CONV_DIM, D_MODEL), f32),
        "w_out": sd * nrm(ks[18], (DEPTH, D_MODEL, D_MODEL), f32),
        "g_final": 1.0 + 0.01 * nrm(ks[19], (D_MODEL,), f32),
    }


def reference(x_prompt, x_sample, cache_k, cache_v, cache_idx_k, state_conv, c_prompt, c_sample,
              w_ada, b_ada, g_norm, w_in, w_a, w_dw, b_dw, g_ln, b_ln, w_b, w_out, g_final):
    xp, xs = x_prompt, x_sample
    bp = xp.shape[0]
    kp_l, vp_l, ikp_l, cp_l, ks_l, vs_l, iks_l, cs_l = [], [], [], [], [], [], [], []
    for l in range(DEPTH):
        lw = (w_ada[l], b_ada[l], g_norm[l], w_in[l], w_a[l], w_dw[l], b_dw[l], g_ln[l], b_ln[l], w_b[l], w_out[l])
        empty_kv = jnp.zeros((bp, 0, N_KV_HEADS, HEAD_DIM), xp.dtype)
        empty_ik = jnp.zeros((bp, 0, IDX_DIM), xp.dtype)
        zero_conv = jnp.zeros((bp, CONV_WIDTH - 1, CONV_DIM), xp.dtype)
        xp, kp, vp, ikp, cp = mixer_layer(xp, c_prompt, empty_kv, empty_kv, empty_ik, zero_conv, *lw)
        xs, k_s, v_s, ik_s, c_s = mixer_layer(xs, c_sample, cache_k[l], cache_v[l], cache_idx_k[l], state_conv[l], *lw)
        kp_l.append(kp); vp_l.append(vp); ikp_l.append(ikp); cp_l.append(cp)
        ks_l.append(k_s); vs_l.append(v_s); iks_l.append(ik_s); cs_l.append(c_s)
    y_prompt = rms_norm(xp, g_final)
    y_sample = rms_norm(xs, g_final)
    return (y_prompt, y_sample,
            jnp.stack(kp_l), jnp.stack(vp_l), jnp.stack(ikp_l), jnp.stack(cp_l),
            jnp.stack(ks_l), jnp.stack(vs_l), jnp.stack(iks_l), jnp.stack(cs_l))
```

```python
import functools

import jax
import jax.numpy as jnp
from jax import lax
from jax.experimental import pallas as pl
from jax.experimental.pallas import tpu as pltpu

CHUNK = 64
CHUNK_SHIFT = CHUNK.bit_length() - 1
assert CHUNK == 1 << CHUNK_SHIFT
N_HEADS = 8
HEAD_DIM = 128
N_KV_HEADS = 2
GROUP = N_HEADS // N_KV_HEADS
ATTN_DIM = N_HEADS * HEAD_DIM
KV_DIM = N_KV_HEADS * HEAD_DIM
IDX_HEADS = 8
IDX_DIM = 64
TOPK_MAX = 256
CONV_WIDTH = 31
HALO = CONV_WIDTH - 1
EPS = 1e-6

LANES = 128
HALO_PAD = 32
NEG_BIG = -1e30
VMEM_LIMIT = 56 * 1024 * 1024
MAX_BISECT = 48

BF16 = jnp.bfloat16
F32 = jnp.float32


def _silu(x):
    return x * jax.nn.sigmoid(x)


def _dot(a, b):
    return jnp.dot(a, b, preferred_element_type=F32)


def _dot_nt(a, b):
    return lax.dot_general(a, b, (((1,), (1,)), ((), ())), preferred_element_type=F32)


def _mod_kernel(c_ref, w_ref, b_ref, g_ref, gmul_ref, shift_ref, gate_ref):
    d = c_ref.shape[1]
    mod = _dot(_silu(c_ref[...]).astype(BF16), w_ref[...]) + b_ref[...]
    shift_ref[...] = mod[:, :d]
    gmul_ref[...] = g_ref[...] * (1.0 + mod[:, d:2 * d])
    gate_ref[...] = mod[:, 2 * d:]


def _modulation(c, w_ada_bf, b_ada, g_norm):
    b, d = c.shape
    out = jax.ShapeDtypeStruct((b, d), F32)
    return pl.pallas_call(
        _mod_kernel,
        name="modulation",
        out_shape=(out, out, out),
        compiler_params=pltpu.CompilerParams(vmem_limit_bytes=VMEM_LIMIT),
    )(c, w_ada_bf, b_ada.reshape(1, -1), g_norm.reshape(1, -1))


def _normed_input(x_ref, gmul_ref, shift_ref):
    x = x_ref[0]
    ms = jnp.mean(x * x, axis=-1, keepdims=True)
    return (x * lax.rsqrt(ms + EPS) * gmul_ref[0] + shift_ref[0]).astype(BF16)


def _qkv_kernel(x_ref, gmul_ref, shift_ref, wqkvg_ref, widx_ref,
                q_ref, k_ref, v_ref, kb_ref, vb_ref, ik_ref, ikb_ref, iqw_ref, iw_ref, sga_ref):
    hb = _normed_input(x_ref, gmul_ref, shift_ref)
    z = _dot(hb, wqkvg_ref[...])
    q_ref[0] = (z[:, :ATTN_DIM] * (HEAD_DIM ** -0.5)).astype(BF16)
    k = z[:, ATTN_DIM:ATTN_DIM + KV_DIM]
    v = z[:, ATTN_DIM + KV_DIM:ATTN_DIM + 2 * KV_DIM]
    k_ref[0] = k
    v_ref[0] = v
    kb_ref[0] = k.astype(BF16)
    vb_ref[0] = v.astype(BF16)
    sga_ref[0] = _silu(z[:, ATTN_DIM + 2 * KV_DIM:]).astype(BF16)
    zi = _dot(hb, widx_ref[...])
    nq = IDX_HEADS * IDX_DIM
    ik = zi[:, nq:nq + IDX_DIM]
    ik_ref[0] = ik
    ikb_ref[0] = ik.astype(BF16)
    iw = zi[:, nq + IDX_DIM:nq + IDX_DIM + IDX_HEADS]
    iw_ref[0] = iw
    wscale = (IDX_HEADS ** -0.5) * (IDX_DIM ** -0.5)
    for h in range(IDX_HEADS):
        iqw_ref[0, h] = (zi[:, h * IDX_DIM:(h + 1) * IDX_DIM] * (iw[:, h:h + 1] * wscale)).astype(BF16)


def _qkv(x, gmul, shift, wqkvg, widx, tb):
    b, t, d = x.shape
    grid = (b, t // tb)
    tok = lambda n, dt: jax.ShapeDtypeStruct((b, t, n), dt)
    blk = lambda n: pl.BlockSpec((1, tb, n), lambda i, j: (i, j, 0))
    vec = pl.BlockSpec((1, 1, d), lambda i, j: (i, 0, 0))
    full = lambda a: pl.BlockSpec(a.shape, lambda i, j: (0,) * a.ndim)
    return pl.pallas_call(
        _qkv_kernel,
        name="qkv_indexer_proj",
        grid=grid,
        in_specs=[blk(d), vec, vec, full(wqkvg), full(widx)],
        out_specs=(blk(ATTN_DIM), blk(KV_DIM), blk(KV_DIM), blk(KV_DIM), blk(KV_DIM),
                   blk(IDX_DIM), blk(IDX_DIM),
                   pl.BlockSpec((1, IDX_HEADS, tb, IDX_DIM), lambda i, j: (i, 0, j, 0)),
                   blk(IDX_HEADS), blk(ATTN_DIM)),
        out_shape=(tok(ATTN_DIM, BF16), tok(KV_DIM, F32), tok(KV_DIM, F32), tok(KV_DIM, BF16),
                   tok(KV_DIM, BF16), tok(IDX_DIM, F32), tok(IDX_DIM, BF16),
                   jax.ShapeDtypeStruct((b, IDX_HEADS, t, IDX_DIM), BF16),
                   tok(IDX_HEADS, F32), tok(ATTN_DIM, BF16)),
        compiler_params=pltpu.CompilerParams(
            dimension_semantics=("arbitrary", "arbitrary"), vmem_limit_bytes=VMEM_LIMIT),
    )(x, gmul, shift, wqkvg, widx)


def _conv_kernel(x_ref, gmul_ref, shift_ref, w2_ref, cs_ref, wdw_ref, bdw_ref, gln_ref, bln_ref, wb_ref,
                 pb_ref, sma_ref, cout_ref, ubuf):
    tb = x_ref.shape[1]
    d = x_ref.shape[2]
    lead = HALO_PAD - HALO

    @pl.when(pl.program_id(1) == 0)
    def _():
        ubuf[0:HALO_PAD, :] = jnp.zeros((HALO_PAD, d), F32)
        ubuf[lead:HALO_PAD, :] = cs_ref[0]

    hb = _normed_input(x_ref, gmul_ref, shift_ref)
    z = _dot(hb, w2_ref[...])
    u = z[:, :d] * jax.nn.sigmoid(z[:, d:2 * d])
    ubuf[HALO_PAD:HALO_PAD + tb, :] = u
    dw = jnp.zeros((tb, d), F32) + bdw_ref[...]
    for j in range(CONV_WIDTH):
        dw = dw + ubuf[lead + j:lead + j + tb, :] * wdw_ref[j:j + 1, :]
    tail = ubuf[tb + lead:tb + HALO_PAD, :]
    cout_ref[0] = tail
    ubuf[lead:HALO_PAD, :] = tail
    mu = jnp.mean(dw, axis=-1, keepdims=True)
    cen = dw - mu
    var = jnp.mean(cen * cen, axis=-1, keepdims=True)
    n = cen * lax.rsqrt(var + EPS) * gln_ref[...] + bln_ref[...]
    yb = _dot((_silu(n) * _silu(z[:, 2 * d:3 * d])).astype(BF16), wb_ref[...])
    sma_ref[0] = jax.nn.sigmoid(z[:, 3 * d:4 * d]).astype(BF16)
    pb_ref[0] = (jax.nn.sigmoid(z[:, 4 * d:5 * d]) * yb).astype(BF16)


def _conv(x, gmul, shift, w2, conv_state, w_dw, b_dw, g_ln, b_ln, wb, tb):
    b, t, d = x.shape
    grid = (b, t // tb)
    blk = pl.BlockSpec((1, tb, d), lambda i, j: (i, j, 0))
    vec = pl.BlockSpec((1, 1, d), lambda i, j: (i, 0, 0))
    halo = pl.BlockSpec((1, HALO, d), lambda i, j: (i, 0, 0))
    full = lambda a: pl.BlockSpec(a.shape, lambda i, j: (0,) * a.ndim)
    row = lambda a: a.reshape(1, -1)
    args = (x, gmul, shift, w2, conv_state, w_dw, row(b_dw), row(g_ln), row(b_ln), wb)
    return pl.pallas_call(
        _conv_kernel,
        name="conv_branch",
        grid=grid,
        in_specs=[blk, vec, vec, full(w2), halo] + [full(a) for a in args[5:]],
        out_specs=(blk, blk, halo),
        out_shape=(jax.ShapeDtypeStruct((b, t, d), BF16), jax.ShapeDtypeStruct((b, t, d), BF16),
                   jax.ShapeDtypeStruct((b, HALO, d), F32)),
        scratch_shapes=[pltpu.VMEM((tb + HALO_PAD, d), F32)],
        compiler_params=pltpu.CompilerParams(
            dimension_semantics=("arbitrary", "arbitrary"), vmem_limit_bytes=VMEM_LIMIT),
    )(*args)


def _lane_partial(x, op):
    acc = x[:, :LANES]
    for c in range(1, x.shape[1] // LANES):
        acc = op(acc, x[:, c * LANES:(c + 1) * LANES])
    return acc


def _attend_kernel(x_ref, gate_ref, gfin_ref, q_ref, iqw_ref, iw_ref, sga_ref, sma_ref, pb_ref,
                   kb_ref, vb_ref, ikb_ref, wa_ref, wout_ref, y_ref, s_ref,
                   *, past, length, topk, lb):
    qb = x_ref.shape[1]
    i = pl.program_id(1)
    qpos0 = past + i * qb
    vis = jnp.minimum(length, (jnp.right_shift(qpos0 + qb - 1, CHUNK_SHIFT) + 1) * CHUNK)
    nkb = lax.div(vis + lb - 1, jnp.int32(lb))
    kf = jnp.float32(topk)

    row = lax.broadcasted_iota(jnp.int32, (qb, lb), 0)
    col = lax.broadcasted_iota(jnp.int32, (qb, lb), 1)
    qchunk = jnp.right_shift(qpos0 + row, CHUNK_SHIFT)

    iw = iw_ref[0]
    lo_b, hi_b, iqs = [], [], []
    for h in range(IDX_HEADS):
        wpos = iw[:, h:h + 1] >= 0.0
        lo_b.append(jnp.where(wpos, 0.0, -jnp.inf))
        hi_b.append(jnp.where(wpos, jnp.inf, 0.0))
        iqs.append(iqw_ref[0, h])

    def score_block(kb, carry):
        rmin, rmax, nadm = carry
        ikblk = ikb_ref[0, pl.ds(pl.multiple_of(kb * lb, lb), lb), :]
        acc = jnp.zeros((qb, lb), F32)
        for h in range(IDX_HEADS):
            acc = acc + jnp.minimum(jnp.maximum(_dot_nt(iqs[h], ikblk), lo_b[h]), hi_b[h])
        kpos = kb * lb + col
        adm = (jnp.right_shift(kpos, CHUNK_SHIFT) <= qchunk) & (kpos < length)
        s_ref[kb] = jnp.where(adm, acc, -jnp.inf)
        rmin = jnp.minimum(rmin, _lane_partial(jnp.where(adm, acc, jnp.inf), jnp.minimum))
        rmax = jnp.maximum(rmax, _lane_partial(jnp.where(adm, acc, -jnp.inf), jnp.maximum))
        nadm = nadm + _lane_partial(jnp.where(adm, 1.0, 0.0), jnp.add)
        return rmin, rmax, nadm

    init = (jnp.full((qb, LANES), jnp.inf, F32), jnp.full((qb, LANES), -jnp.inf, F32),
            jnp.zeros((qb, LANES), F32))
    rmin, rmax, nadm = lax.fori_loop(0, nkb, score_block, init)
    rmin = jnp.min(rmin, axis=-1, keepdims=True)
    rmax = jnp.max(rmax, axis=-1, keepdims=True)
    nadm = jnp.sum(nadm, axis=-1, keepdims=True)

    def count_ge(t):
        def body(kb, acc):
            return acc + _lane_partial(jnp.where(s_ref[kb] >= t, 1.0, 0.0), jnp.add)
        acc = lax.fori_loop(0, nkb, body, jnp.zeros((qb, LANES), F32))
        return jnp.sum(acc, axis=-1, keepdims=True)

    def unsettled(clo):
        return jnp.max(jnp.where(clo > kf, 1.0, 0.0)) > 0.0

    def bisect_cond(st):
        it, lo, hi, clo = st
        return (it < MAX_BISECT) & unsettled(clo)

    def bisect_body(st):
        it, lo, hi, clo = st
        mid = lo + (hi - lo) * 0.5
        c = count_ge(mid)
        up = (c >= kf) & (clo > kf)
        return (it + 1, jnp.where(up, mid, lo), jnp.where(up | (clo <= kf), hi, mid),
                jnp.where(up, c, clo))

    _, lo, hi, clo = lax.while_loop(bisect_cond, bisect_body, (jnp.int32(0), rmin, rmax, nadm))
    thr_ref = lo

    def tie_fallback():
        def snap_lo(t):
            def body(kb, acc):
                s = s_ref[kb]
                return jnp.minimum(acc, _lane_partial(jnp.where(s >= t, s, jnp.inf), jnp.minimum))
            acc = lax.fori_loop(0, nkb, body, jnp.full((qb, LANES), jnp.inf, F32))
            return jnp.min(acc, axis=-1, keepdims=True)

        def snap_hi(t):
            def body(kb, acc):
                s = s_ref[kb]
                return jnp.maximum(acc, _lane_partial(jnp.where(s <= t, s, -jnp.inf), jnp.maximum))
            acc = lax.fori_loop(0, nkb, body, jnp.full((qb, LANES), -jnp.inf, F32))
            return jnp.max(acc, axis=-1, keepdims=True)

        def count_gt(t):
            def body(kb, acc):
                return acc + _lane_partial(jnp.where(s_ref[kb] > t, 1.0, 0.0), jnp.add)
            acc = lax.fori_loop(0, nkb, body, jnp.zeros((qb, LANES), F32))
            return jnp.sum(acc, axis=-1, keepdims=True)

        def above(t):
            def body(kb, acc):
                s = s_ref[kb]
                return jnp.minimum(acc, _lane_partial(jnp.where(s > t, s, jnp.inf), jnp.minimum))
            acc = lax.fori_loop(0, nkb, body, jnp.full((qb, LANES), jnp.inf, F32))
            return jnp.min(acc, axis=-1, keepdims=True)

        def snap_cond(st):
            a, b = st
            return jnp.max(jnp.where(a < b, 1.0, 0.0)) > 0.0

        def snap_body(st):
            a, b = st
            mid = a + (b - a) * 0.5
            mid = jnp.where(mid < b, mid, a)
            more = count_gt(mid) >= kf
            return (jnp.where(more & (a < b), above(mid), a),
                    jnp.where(more | (a >= b), b, snap_hi(mid)))

        a, _ = lax.while_loop(snap_cond, snap_body, (snap_lo(lo), snap_hi(hi)))
        want = kf - count_gt(a)
        r_i = lax.broadcasted_iota(jnp.int32, (lb, lb), 0)
        c_i = lax.broadcasted_iota(jnp.int32, (lb, lb), 1)
        tri = jnp.where(r_i <= c_i, 1.0, 0.0).astype(BF16)

        def drop_body(kb, seen):
            s = s_ref[kb]
            tie = s == a
            tie_f = jnp.where(tie, 1.0, 0.0)
            rank = seen + _dot(tie_f.astype(BF16), tri) - tie_f
            s_ref[kb] = jnp.where(tie & (rank >= want), -jnp.inf, s)
            return seen + jnp.sum(tie_f, axis=-1, keepdims=True)

        lax.fori_loop(0, nkb, drop_body, jnp.zeros((qb, 1), F32))
        return a

    thr = lax.cond(unsettled(clo), tie_fallback, lambda: thr_ref)

    q = q_ref[0]
    outs = []
    for g in range(N_KV_HEADS):
        qg = jnp.concatenate(
            [q[:, (g * GROUP + r) * HEAD_DIM:(g * GROUP + r + 1) * HEAD_DIM] for r in range(GROUP)], axis=0)

        def attn_block(kb, carry, g=g, qg=qg):
            m, l, acc = carry
            start = pl.multiple_of(kb * lb, lb)
            kblk = kb_ref[0, pl.ds(start, lb), g * HEAD_DIM:(g + 1) * HEAD_DIM]
            vblk = vb_ref[0, pl.ds(start, lb), g * HEAD_DIM:(g + 1) * HEAD_DIM]
            sel = s_ref[kb] >= thr
            s = jnp.where(sel[None], _dot_nt(qg, kblk).reshape(GROUP, qb, lb), NEG_BIG)
            m_new = jnp.maximum(m, jnp.max(s, axis=-1, keepdims=True))
            alpha = jnp.exp(m - m_new)
            p = jnp.exp(s - m_new)
            l = alpha * l + jnp.sum(p, axis=-1, keepdims=True)
            pv = _dot(p.astype(BF16).reshape(GROUP * qb, lb), vblk).reshape(GROUP, qb, HEAD_DIM)
            return m_new, l, alpha * acc + pv

        init = (jnp.full((GROUP, qb, 1), NEG_BIG, F32), jnp.zeros((GROUP, qb, 1), F32),
                jnp.zeros((GROUP, qb, HEAD_DIM), F32))
        _, l, acc = lax.fori_loop(0, nkb, attn_block, init)
        o = acc / l
        outs.extend(o[r] for r in range(GROUP))
    o_a = jnp.concatenate(outs, axis=-1)

    ya = _dot((o_a * sga_ref[0].astype(F32)).astype(BF16), wa_ref[...])
    mixed = sma_ref[0].astype(F32) * ya + pb_ref[0].astype(F32)
    xo = x_ref[0] + gate_ref[0] * _dot(mixed.astype(BF16), wout_ref[...])
    ms = jnp.mean(xo * xo, axis=-1, keepdims=True)
    y_ref[0] = xo * lax.rsqrt(ms + EPS) * gfin_ref[...]


def _attend(x, gate, g_final, q, iqw, iw, sga, sma, pb, kall, vall, ikall, wa, wout,
            *, past, length, qb, lb):
    b, t, d = x.shape
    lp = kall.shape[1]
    topk = min(TOPK_MAX, length // 4)
    grid = (b, t // qb)
    blk = lambda n: pl.BlockSpec((1, qb, n), lambda i, j: (i, j, 0))
    vec = pl.BlockSpec((1, 1, d), lambda i, j: (i, 0, 0))
    keys = lambda n: pl.BlockSpec((1, lp, n), lambda i, j: (i, 0, 0))
    full = lambda a: pl.BlockSpec(a.shape, lambda i, j: (0,) * a.ndim)
    gfin = g_final.reshape(1, -1)
    kern = functools.partial(_attend_kernel, past=past, length=length, topk=topk, lb=lb)
    return pl.pallas_call(
        kern,
        name="sparse_attend_out",
        grid=grid,
        in_specs=[blk(d), vec, full(gfin), blk(ATTN_DIM),
                  pl.BlockSpec((1, IDX_HEADS, qb, IDX_DIM), lambda i, j: (i, 0, j, 0)),
                  blk(IDX_HEADS), blk(ATTN_DIM), blk(d), blk(d),
                  keys(KV_DIM), keys(KV_DIM), keys(IDX_DIM), full(wa), full(wout)],
        out_specs=blk(d),
        out_shape=jax.ShapeDtypeStruct((b, t, d), F32),
        scratch_shapes=[pltpu.VMEM((lp // lb, qb, lb), F32)],
        compiler_params=pltpu.CompilerParams(
            dimension_semantics=("arbitrary", "arbitrary"), vmem_limit_bytes=VMEM_LIMIT),
    )(x, gate, gfin, q, iqw, iw, sga, sma, pb, kall, vall, ikall, wa, wout)


def _block_sizes(t):
    tb = 256 if t % 256 == 0 else t
    qb = 128 if t % 128 == 0 else t
    return tb, qb, 512


def _split_weights(w_in, d):
    o_iq = 2 * ATTN_DIM + 2 * KV_DIM
    o_iw = o_iq + IDX_HEADS * IDX_DIM
    o_ik = o_iw + IDX_HEADS
    o_glu = o_ik + IDX_DIM
    nidx = IDX_HEADS * IDX_DIM + IDX_DIM + IDX_HEADS
    pad = (-nidx) % LANES
    wqkvg = w_in[:, :o_iq].astype(BF16)
    widx = jnp.concatenate([w_in[:, o_iq:o_iw], w_in[:, o_ik:o_glu], w_in[:, o_iw:o_ik],
                            jnp.zeros((d, pad), w_in.dtype)], axis=1).astype(BF16)
    w2 = w_in[:, o_glu:].astype(BF16)
    return wqkvg, widx, w2


def _layer(x, c, past_k, past_v, past_ik, conv_state, wts, g_final):
    w_ada_bf, b_ada, g_norm, wqkvg, widx, w2, wa, w_dw, b_dw, g_ln, b_ln, wb, wout = wts
    b, t, d = x.shape
    assert t >= HALO and t % 8 == 0
    tb, qb, lb = _block_sizes(t)
    gmul, shift, gate = (m.reshape(b, 1, d) for m in _modulation(c, w_ada_bf, b_ada, g_norm))
    q, k, v, kb, vb, ik, ikb, iqw, iw, sga = _qkv(x, gmul, shift, wqkvg, widx, tb)
    pb, sma, new_conv = _conv(x, gmul, shift, w2, conv_state, w_dw, b_dw, g_ln, b_ln, wb, tb)
    past = 0 if past_k is None else past_k.shape[1]
    length = past + t
    if past:
        kb = jnp.concatenate([past_k, kb], axis=1)
        vb = jnp.concatenate([past_v, vb], axis=1)
        ikb = jnp.concatenate([past_ik, ikb], axis=1)
    pad = (-length) % lb
    if pad:
        kb, vb, ikb = (jnp.pad(a, ((0, 0), (0, pad), (0, 0))) for a in (kb, vb, ikb))
    y = _attend(x, gate, g_final, q, iqw, iw, sga, sma, pb, kb, vb, ikb, wa, wout,
                past=past, length=length, qb=qb, lb=lb)
    return y, k, v, ik, new_conv


def kernel(x_prompt, x_sample, cache_k, cache_v, cache_idx_k, state_conv, c_prompt, c_sample,
           w_ada, b_ada, g_norm, w_in, w_a, w_dw, b_dw, g_ln, b_ln, w_b, w_out, g_final):
    depth = w_in.shape[0]
    assert depth == 1, "single-layer trunk"
    d = x_prompt.shape[-1]
    bp, tp = x_prompt.shape[:2]
    bs, ts = x_sample.shape[:2]
    wqkvg, widx, w2 = _split_weights(w_in[0], d)
    wts = (w_ada[0].astype(BF16), b_ada[0], g_norm[0], wqkvg, widx, w2, w_a[0].astype(BF16),
           w_dw[0], b_dw[0], g_ln[0], b_ln[0], w_b[0].astype(BF16), w_out[0].astype(BF16))
    zero_conv = jnp.zeros((bp, HALO, d), x_prompt.dtype)
    yp, kp, vp, ikp, cp = _layer(x_prompt, c_prompt, None, None, None, zero_conv, wts, g_final)
    pk = cache_k[0].reshape(bs, -1, KV_DIM).astype(BF16)
    pv = cache_v[0].reshape(bs, -1, KV_DIM).astype(BF16)
    pik = cache_idx_k[0].astype(BF16)
    ys, k_s, v_s, ik_s, c_s = _layer(x_sample, c_sample, pk, pv, pik, state_conv[0], wts, g_final)
    heads = lambda a: a.reshape(1, a.shape[0], a.shape[1], N_KV_HEADS, HEAD_DIM)
    return (yp, ys, heads(kp), heads(vp), ikp[None], cp[None],
            heads(k_s), heads(v_s), ik_s[None], c_s[None])
```

```python
import functools

import jax
import jax.numpy as jnp
from jax import lax
from jax.experimental import pallas as pl
from jax.experimental.pallas import tpu as pltpu

CHUNK = 64
CHUNK_SHIFT = CHUNK.bit_length() - 1
assert CHUNK == 1 << CHUNK_SHIFT
N_HEADS = 8
HEAD_DIM = 128
N_KV_HEADS = 2
GROUP = N_HEADS // N_KV_HEADS
ATTN_DIM = N_HEADS * HEAD_DIM
KV_DIM = N_KV_HEADS * HEAD_DIM
IDX_HEADS = 8
IDX_DIM = 64
TOPK_MAX = 256
CONV_WIDTH = 31
HALO = CONV_WIDTH - 1
EPS = 1e-6

LANES = 128
SUBLANES = 8
HALO_PAD = 32
NEG_BIG = -1e30
FLT_BIG = 3e38
LOG2E = 1.4426950408889634
VMEM_LIMIT = 56 * 1024 * 1024
MAX_BISECT = 40
ATTN_STACK = 2
assert GROUP % ATTN_STACK == 0

BF16 = jnp.bfloat16
F32 = jnp.float32


def _silu(x):
    return x * jax.nn.sigmoid(x)


def _dot(a, b):
    return jnp.dot(a, b, preferred_element_type=F32)


def _dot_nt(a, b):
    return lax.dot_general(a, b, (((1,), (1,)), ((), ())), preferred_element_type=F32)


def _mod_kernel(c_ref, w_ref, b_ref, g_ref, gmul_ref, shift_ref, gate_ref):
    d = c_ref.shape[1]
    mod = _dot(_silu(c_ref[...]).astype(BF16), w_ref[...]) + b_ref[...]
    shift_ref[...] = mod[:, :d]
    gmul_ref[...] = g_ref[...] * (1.0 + mod[:, d:2 * d])
    gate_ref[...] = mod[:, 2 * d:]


def _modulation(c, w_ada_bf, b_ada, g_norm):
    b, d = c.shape
    out = jax.ShapeDtypeStruct((b, d), F32)
    return pl.pallas_call(
        _mod_kernel,
        name="modulation",
        out_shape=(out, out, out),
        compiler_params=pltpu.CompilerParams(vmem_limit_bytes=VMEM_LIMIT),
    )(c, w_ada_bf, b_ada.reshape(1, -1), g_norm.reshape(1, -1))


def _normed_input(x_ref, gmul_ref, shift_ref):
    x = x_ref[0]
    ms = jnp.mean(x * x, axis=-1, keepdims=True)
    return (x * lax.rsqrt(ms + EPS) * gmul_ref[0] + shift_ref[0]).astype(BF16)


def _qkv_kernel(x_ref, gmul_ref, shift_ref, wqkvg_ref, widx_ref,
                q_ref, k_ref, v_ref, kb_ref, vb_ref, ik_ref, ikb_ref, iqw_ref, iw_ref, sga_ref):
    hb = _normed_input(x_ref, gmul_ref, shift_ref)
    z = _dot(hb, wqkvg_ref[...])
    q_ref[0] = (z[:, :ATTN_DIM] * (HEAD_DIM ** -0.5 * LOG2E)).astype(BF16)
    k = z[:, ATTN_DIM:ATTN_DIM + KV_DIM]
    v = z[:, ATTN_DIM + KV_DIM:ATTN_DIM + 2 * KV_DIM]
    k_ref[0] = k
    v_ref[0] = v
    kb_ref[0] = k.astype(BF16)
    vb_ref[0] = v.astype(BF16)
    sga_ref[0] = _silu(z[:, ATTN_DIM + 2 * KV_DIM:]).astype(BF16)
    zi = _dot(hb, widx_ref[...])
    nq = IDX_HEADS * IDX_DIM
    ik = zi[:, nq:nq + IDX_DIM]
    ik_ref[0] = ik
    ikb_ref[0] = ik.astype(BF16)
    iw = zi[:, nq + IDX_DIM:nq + IDX_DIM + IDX_HEADS]
    iw_ref[0] = iw
    wscale = (IDX_HEADS ** -0.5) * (IDX_DIM ** -0.5)
    for h in range(IDX_HEADS):
        iqw_ref[0, h] = (zi[:, h * IDX_DIM:(h + 1) * IDX_DIM] * (iw[:, h:h + 1] * wscale)).astype(BF16)


def _qkv(x, gmul, shift, wqkvg, widx, tb):
    b, t, d = x.shape
    grid = (b, t // tb)
    tok = lambda n, dt: jax.ShapeDtypeStruct((b, t, n), dt)
    blk = lambda n: pl.BlockSpec((1, tb, n), lambda i, j: (i, j, 0))
    vec = pl.BlockSpec((1, 1, d), lambda i, j: (i, 0, 0))
    full = lambda a: pl.BlockSpec(a.shape, lambda i, j: (0,) * a.ndim)
    return pl.pallas_call(
        _qkv_kernel,
        name="qkv_indexer_proj",
        grid=grid,
        in_specs=[blk(d), vec, vec, full(wqkvg), full(widx)],
        out_specs=(blk(ATTN_DIM), blk(KV_DIM), blk(KV_DIM), blk(KV_DIM), blk(KV_DIM),
                   blk(IDX_DIM), blk(IDX_DIM),
                   pl.BlockSpec((1, IDX_HEADS, tb, IDX_DIM), lambda i, j: (i, 0, j, 0)),
                   blk(IDX_HEADS), blk(ATTN_DIM)),
        out_shape=(tok(ATTN_DIM, BF16), tok(KV_DIM, F32), tok(KV_DIM, F32), tok(KV_DIM, BF16),
                   tok(KV_DIM, BF16), tok(IDX_DIM, F32), tok(IDX_DIM, BF16),
                   jax.ShapeDtypeStruct((b, IDX_HEADS, t, IDX_DIM), BF16),
                   tok(IDX_HEADS, F32), tok(ATTN_DIM, BF16)),
        compiler_params=pltpu.CompilerParams(
            dimension_semantics=("arbitrary", "arbitrary"), vmem_limit_bytes=VMEM_LIMIT),
    )(x, gmul, shift, wqkvg, widx)


def _conv_kernel(x_ref, gmul_ref, shift_ref, w2_ref, cs_ref, wdw_ref, bdw_ref, gln_ref, bln_ref, wb_ref,
                 pb_ref, sma_ref, cout_ref, ubuf, shbuf):
    tb = x_ref.shape[1]
    d = x_ref.shape[2]
    lead = HALO_PAD - HALO

    @pl.when(pl.program_id(1) == 0)
    def _():
        ubuf[0:HALO_PAD, :] = jnp.zeros((HALO_PAD, d), F32)
        ubuf[lead:HALO_PAD, :] = cs_ref[0]

    hb = _normed_input(x_ref, gmul_ref, shift_ref)
    z = _dot(hb, w2_ref[...])
    u = z[:, :d] * jax.nn.sigmoid(z[:, d:2 * d])
    ubuf[HALO_PAD:HALO_PAD + tb, :] = u
    dw = jnp.zeros((tb, d), F32) + bdw_ref[...]
    for r in range(SUBLANES):
        taps = [j for j in range(CONV_WIDTH) if (lead + j) % SUBLANES == r]
        span = lead + taps[-1] - r + tb
        src = ubuf
        if r:
            shbuf[0:span, :] = ubuf[r:r + span, :]
            src = shbuf
        for j in taps:
            off = lead + j - r
            dw = dw + src[off:off + tb, :] * wdw_ref[j:j + 1, :]
    tail = ubuf[tb + lead:tb + HALO_PAD, :]
    cout_ref[0] = tail
    ubuf[lead:HALO_PAD, :] = tail
    mu = jnp.mean(dw, axis=-1, keepdims=True)
    cen = dw - mu
    var = jnp.mean(cen * cen, axis=-1, keepdims=True)
    n = cen * lax.rsqrt(var + EPS) * gln_ref[...] + bln_ref[...]
    yb = _dot((_silu(n) * _silu(z[:, 2 * d:3 * d])).astype(BF16), wb_ref[...])
    sma_ref[0] = jax.nn.sigmoid(z[:, 3 * d:4 * d]).astype(BF16)
    pb_ref[0] = (jax.nn.sigmoid(z[:, 4 * d:5 * d]) * yb).astype(BF16)


def _conv(x, gmul, shift, w2, conv_state, w_dw, b_dw, g_ln, b_ln, wb, tb):
    b, t, d = x.shape
    grid = (b, t // tb)
    blk = pl.BlockSpec((1, tb, d), lambda i, j: (i, j, 0))
    vec = pl.BlockSpec((1, 1, d), lambda i, j: (i, 0, 0))
    halo = pl.BlockSpec((1, HALO, d), lambda i, j: (i, 0, 0))
    full = lambda a: pl.BlockSpec(a.shape, lambda i, j: (0,) * a.ndim)
    row = lambda a: a.reshape(1, -1)
    args = (x, gmul, shift, w2, conv_state, w_dw, row(b_dw), row(g_ln), row(b_ln), wb)
    return pl.pallas_call(
        _conv_kernel,
        name="conv_branch",
        grid=grid,
        in_specs=[blk, vec, vec, full(w2), halo] + [full(a) for a in args[5:]],
        out_specs=(blk, blk, halo),
        out_shape=(jax.ShapeDtypeStruct((b, t, d), BF16), jax.ShapeDtypeStruct((b, t, d), BF16),
                   jax.ShapeDtypeStruct((b, HALO, d), F32)),
        scratch_shapes=[pltpu.VMEM((tb + HALO_PAD, d), F32), pltpu.VMEM((tb + HALO_PAD, d), F32)],
        compiler_params=pltpu.CompilerParams(
            dimension_semantics=("arbitrary", "arbitrary"), vmem_limit_bytes=VMEM_LIMIT),
    )(*args)


def _lane_partial(x, op):
    acc = x[:, :LANES]
    for c in range(1, x.shape[1] // LANES):
        acc = op(acc, x[:, c * LANES:(c + 1) * LANES])
    return acc


def _attend_kernel(x_ref, gate_ref, gfin_ref, q_ref, iqw_ref, iw_ref, sga_ref, sma_ref, pb_ref,
                   kt_ref, vb_ref, ikb_ref, wa_ref, wout_ref, y_ref, s_ref, bias_ref,
                   *, past, length, topk, lb):
    qb = x_ref.shape[1]
    i = pl.program_id(1)
    qpos0 = past + i * qb
    kf = jnp.float32(topk)

    def visible(qpos):
        return jnp.minimum(length, (jnp.right_shift(qpos, CHUNK_SHIFT) + 1) * CHUNK)

    nfull = lax.div(visible(qpos0), jnp.int32(lb))
    nkb = lax.div(visible(qpos0 + qb - 1) + lb - 1, jnp.int32(lb))
    nadm = visible(qpos0 + lax.broadcasted_iota(jnp.int32, (qb, 1), 0)).astype(F32)

    iw = iw_ref[0]
    lo_b, hi_b, iqs = [], [], []
    for h in range(IDX_HEADS):
        wpos = iw[:, h:h + 1] >= 0.0
        lo_b.append(jnp.where(wpos, 0.0, -jnp.inf))
        hi_b.append(jnp.where(wpos, jnp.inf, 0.0))
        iqs.append(iqw_ref[0, h])

    def score(kb):
        ikblk = ikb_ref[0, pl.ds(pl.multiple_of(kb * lb, lb), lb), :]
        acc = jnp.minimum(jnp.maximum(_dot_nt(iqs[0], ikblk), lo_b[0]), hi_b[0])
        for h in range(1, IDX_HEADS):
            acc = acc + jnp.minimum(jnp.maximum(_dot_nt(iqs[h], ikblk), lo_b[h]), hi_b[h])
        return acc

    def whole_block(kb, carry):
        rmin, rmax = carry
        acc = score(kb)
        s_ref[kb] = acc
        return (jnp.minimum(rmin, _lane_partial(acc, jnp.minimum)),
                jnp.maximum(rmax, _lane_partial(acc, jnp.maximum)))

    def edge_block(kb, carry):
        rmin, rmax = carry
        acc = score(kb)
        kpos = kb * lb + lax.broadcasted_iota(jnp.int32, (qb, lb), 1)
        adm = kpos.astype(F32) < nadm
        s_ref[kb] = jnp.where(adm, acc, -jnp.inf)
        return (jnp.minimum(rmin, _lane_partial(jnp.where(adm, acc, jnp.inf), jnp.minimum)),
                jnp.maximum(rmax, _lane_partial(jnp.where(adm, acc, -jnp.inf), jnp.maximum)))

    ext = (jnp.full((qb, LANES), jnp.inf, F32), jnp.full((qb, LANES), -jnp.inf, F32))
    ext = lax.fori_loop(0, nfull, whole_block, ext)
    rmin, rmax = lax.fori_loop(nfull, nkb, edge_block, ext)
    rmin = jnp.min(rmin, axis=-1, keepdims=True)
    rmax = jnp.max(rmax, axis=-1, keepdims=True)

    def fold(pick, op, start):
        def body(kb, acc):
            for c in range(lb // LANES):
                acc = op(acc, pick(s_ref[kb, :, c * LANES:(c + 1) * LANES]))
            return acc
        return lax.fori_loop(0, nkb, body, jnp.full((qb, LANES), start, F32))

    def count(pred):
        acc = fold(lambda s: jnp.where(pred(s), 1.0, 0.0), jnp.add, 0.0)
        return jnp.sum(acc, axis=-1, keepdims=True)

    def any_row(flag):
        return jnp.max(jnp.where(flag, 1.0, 0.0)) > 0.0

    cge0 = count(lambda s: s >= 0.0)
    cgt0 = count(lambda s: s > 0.0)
    take_all = nadm <= kf
    tie0 = jnp.logical_not(take_all) & (cgt0 < kf) & (cge0 >= kf)
    pos = cgt0 >= kf
    lo = jnp.where(pos, 0.0, rmin)
    hi = jnp.where(pos, rmax, 0.0)
    done = take_all | tie0 | (jnp.where(pos, cge0, nadm) == kf)

    def bisect_cond(st):
        it, lo, hi, open_ = st
        return (it < MAX_BISECT) & (jnp.max(open_) > 0.0)

    def bisect_body(st):
        it, lo, hi, open_ = st
        mid = lo + (hi - lo) * 0.5
        c = count(lambda s: s >= mid)
        up = (open_ > 0.0) & (c >= kf)
        down = (open_ > 0.0) & (c < kf)
        return (it + 1, jnp.where(up, mid, lo), jnp.where(down, mid, hi),
                jnp.where(up & (c == kf), 0.0, open_))

    _, lo, hi, open_ = lax.while_loop(bisect_cond, bisect_body,
                                      (jnp.int32(0), lo, hi, jnp.where(done, 0.0, 1.0)))
    done = open_ <= 0.0
    thr = jnp.where(take_all, -FLT_BIG, jnp.where(tie0, 0.0, lo))
    ties = jnp.where(tie0, kf - cgt0, jnp.inf)

    def snap_threshold():
        def snap_cond(st):
            a, b = st
            return any_row(a < b)

        def snap_body(st):
            a, b = st
            mid = a + (b - a) * 0.5
            mid = jnp.where(mid < b, mid, a)
            more = count(lambda s: s > mid) >= kf
            above = jnp.min(fold(lambda s: jnp.where(s > mid, s, jnp.inf), jnp.minimum, jnp.inf),
                            axis=-1, keepdims=True)
            below = jnp.max(fold(lambda s: jnp.where(s <= mid, s, -jnp.inf), jnp.maximum, -jnp.inf),
                            axis=-1, keepdims=True)
            return jnp.where(more & (a < b), above, a), jnp.where(more | (a >= b), b, below)

        a0 = jnp.min(fold(lambda s: jnp.where(s >= lo, s, jnp.inf), jnp.minimum, jnp.inf),
                     axis=-1, keepdims=True)
        b0 = jnp.max(fold(lambda s: jnp.where(s <= hi, s, -jnp.inf), jnp.maximum, -jnp.inf),
                     axis=-1, keepdims=True)
        a, _ = lax.while_loop(snap_cond, snap_body, (jnp.where(done, thr, a0), jnp.where(done, thr, b0)))
        want = kf - count(lambda s: s > a)
        return jnp.where(done, thr, a), jnp.where(done, ties, want)

    thr, ties = lax.cond(any_row(jnp.logical_not(done)), snap_threshold, lambda: (thr, ties))

    def mask_with_ties():
        r_i = lax.broadcasted_iota(jnp.int32, (LANES, LANES), 0)
        c_i = lax.broadcasted_iota(jnp.int32, (LANES, LANES), 1)
        tri = jnp.where(r_i <= c_i, 1.0, 0.0).astype(BF16)

        def body(kb, seen):
            s = s_ref[kb]
            cols = []
            for c in range(lb // LANES):
                sc = s[:, c * LANES:(c + 1) * LANES]
                eq = sc == thr
                eq_f = jnp.where(eq, 1.0, 0.0)
                before = seen + _dot(eq_f.astype(BF16), tri) - eq_f
                keep = (sc > thr) | (eq & (before < ties))
                cols.append(jnp.where(keep, 0.0, NEG_BIG).astype(BF16))
                seen = seen + jnp.sum(eq_f, axis=-1, keepdims=True)
            bias_ref[kb] = jnp.concatenate(cols, axis=1)
            return seen

        lax.fori_loop(0, nkb, body, jnp.zeros((qb, 1), F32))

    def mask_plain():
        def body(kb, carry):
            bias_ref[kb] = jnp.where(s_ref[kb] >= thr, 0.0, NEG_BIG).astype(BF16)
            return carry

        lax.fori_loop(0, nkb, body, jnp.int32(0))

    lax.cond(any_row(ties < jnp.inf), mask_with_ties, mask_plain)

    q = q_ref[0]
    eye = jnp.where(lax.broadcasted_iota(jnp.int32, (qb, qb), 0) == lax.broadcasted_iota(jnp.int32, (qb, qb), 1),
                    1.0, 0.0).astype(BF16)
    ones = jnp.ones((lb, HEAD_DIM), BF16)
    stacks = [(h0 // GROUP, h0) for h0 in range(0, N_HEADS, ATTN_STACK)]
    lhs = [jnp.concatenate(
        [jnp.concatenate([q[:, h * HEAD_DIM:(h + 1) * HEAD_DIM], eye], axis=1)
         for h in range(h0, h0 + ATTN_STACK)], axis=0) for _, h0 in stacks]

    def attn_block(kb, carry):
        bias = bias_ref[kb]
        vrows = vb_ref[0, pl.ds(pl.multiple_of(kb * lb, lb), lb), :]
        rhs = [jnp.concatenate([kt_ref[0, kb, g * HEAD_DIM:(g + 1) * HEAD_DIM, :], bias], axis=0)
               for g in range(N_KV_HEADS)]
        vext = [jnp.concatenate([vrows[:, g * HEAD_DIM:(g + 1) * HEAD_DIM], ones], axis=1)
                for g in range(N_KV_HEADS)]
        logits = [_dot(lhs_c, rhs[g]) for (g, _), lhs_c in zip(stacks, lhs)]
        m_new = [jnp.maximum(m, jnp.max(s, axis=-1, keepdims=True)) for s, (m, _) in zip(logits, carry)]
        probs = [jnp.exp2(s - mn).astype(BF16) for s, mn in zip(logits, m_new)]
        pv = [_dot(p, vext[g]) for (g, _), p in zip(stacks, probs)]
        return tuple((mn, jnp.exp2(m - mn) * acc + o) for mn, (m, acc), o in zip(m_new, carry, pv))

    init = tuple((jnp.full((ATTN_STACK * qb, 1), NEG_BIG, F32),
                  jnp.zeros((ATTN_STACK * qb, 2 * HEAD_DIM), F32)) for _ in stacks)
    res = lax.fori_loop(0, nkb, attn_block, init)
    outs = []
    for _, acc in res:
        o = acc[:, :HEAD_DIM] / acc[:, HEAD_DIM:HEAD_DIM + 1]
        outs.extend(o[r * qb:(r + 1) * qb] for r in range(ATTN_STACK))
    o_a = jnp.concatenate(outs, axis=-1)

    ya = _dot((o_a * sga_ref[0].astype(F32)).astype(BF16), wa_ref[...])
    mixed = sma_ref[0].astype(F32) * ya + pb_ref[0].astype(F32)
    xo = x_ref[0] + gate_ref[0] * _dot(mixed.astype(BF16), wout_ref[...])
    ms = jnp.mean(xo * xo, axis=-1, keepdims=True)
    y_ref[0] = xo * lax.rsqrt(ms + EPS) * gfin_ref[...]


def _attend(x, gate, g_final, q, iqw, iw, sga, sma, pb, kall, vall, ikall, wa, wout,
            *, past, length, qb, lb):
    b, t, d = x.shape
    lp = kall.shape[1]
    nkb = lp // lb
    topk = min(TOPK_MAX, length // 4)
    grid = (b, t // qb)
    kt = kall.reshape(b, nkb, lb, KV_DIM).transpose(0, 1, 3, 2)
    blk = lambda n: pl.BlockSpec((1, qb, n), lambda i, j: (i, j, 0))
    vec = pl.BlockSpec((1, 1, d), lambda i, j: (i, 0, 0))
    keys = lambda n: pl.BlockSpec((1, lp, n), lambda i, j: (i, 0, 0))
    full = lambda a: pl.BlockSpec(a.shape, lambda i, j: (0,) * a.ndim)
    gfin = g_final.reshape(1, -1)
    kern = functools.partial(_attend_kernel, past=past, length=length, topk=topk, lb=lb)
    return pl.pallas_call(
        kern,
        name="sparse_attend_out",
        grid=grid,
        in_specs=[blk(d), vec, full(gfin), blk(ATTN_DIM),
                  pl.BlockSpec((1, IDX_HEADS, qb, IDX_DIM), lambda i, j: (i, 0, j, 0)),
                  blk(IDX_HEADS), blk(ATTN_DIM), blk(d), blk(d),
                  pl.BlockSpec((1, nkb, KV_DIM, lb), lambda i, j: (i, 0, 0, 0)),
                  keys(KV_DIM), keys(IDX_DIM), full(wa), full(wout)],
        out_specs=blk(d),
        out_shape=jax.ShapeDtypeStruct((b, t, d), F32),
        scratch_shapes=[pltpu.VMEM((nkb, qb, lb), F32), pltpu.VMEM((nkb, qb, lb), BF16)],
        compiler_params=pltpu.CompilerParams(
            dimension_semantics=("arbitrary", "arbitrary"), vmem_limit_bytes=VMEM_LIMIT),
    )(x, gate, gfin, q, iqw, iw, sga, sma, pb, kt, vall, ikall, wa, wout)


def _block_sizes(t):
    tb = 256 if t % 256 == 0 else t
    qb = 128 if t % 128 == 0 else t
    return tb, qb, 512


def _split_weights(w_in, d):
    o_iq = 2 * ATTN_DIM + 2 * KV_DIM
    o_iw = o_iq + IDX_HEADS * IDX_DIM
    o_ik = o_iw + IDX_HEADS
    o_glu = o_ik + IDX_DIM
    nidx = IDX_HEADS * IDX_DIM + IDX_DIM + IDX_HEADS
    pad = (-nidx) % LANES
    wqkvg = w_in[:, :o_iq].astype(BF16)
    widx = jnp.concatenate([w_in[:, o_iq:o_iw], w_in[:, o_ik:o_glu], w_in[:, o_iw:o_ik],
                            jnp.zeros((d, pad), w_in.dtype)], axis=1).astype(BF16)
    w2 = w_in[:, o_glu:].astype(BF16)
    return wqkvg, widx, w2


def _layer(x, c, past_k, past_v, past_ik, conv_state, wts, g_final):
    w_ada_bf, b_ada, g_norm, wqkvg, widx, w2, wa, w_dw, b_dw, g_ln, b_ln, wb, wout = wts
    b, t, d = x.shape
    assert t >= HALO and t % 8 == 0
    tb, qb, lb = _block_sizes(t)
    gmul, shift, gate = (m.reshape(b, 1, d) for m in _modulation(c, w_ada_bf, b_ada, g_norm))
    q, k, v, kb, vb, ik, ikb, iqw, iw, sga = _qkv(x, gmul, shift, wqkvg, widx, tb)
    pb, sma, new_conv = _conv(x, gmul, shift, w2, conv_state, w_dw, b_dw, g_ln, b_ln, wb, tb)
    past = 0 if past_k is None else past_k.shape[1]
    length = past + t
    if past:
        kb = jnp.concatenate([past_k, kb], axis=1)
        vb = jnp.concatenate([past_v, vb], axis=1)
        ikb = jnp.concatenate([past_ik, ikb], axis=1)
    pad = (-length) % lb
    if pad:
        kb, vb, ikb = (jnp.pad(a, ((0, 0), (0, pad), (0, 0))) for a in (kb, vb, ikb))
    y = _attend(x, gate, g_final, q, iqw, iw, sga, sma, pb, kb, vb, ikb, wa, wout,
                past=past, length=length, qb=qb, lb=lb)
    return y, k, v, ik, new_conv


def kernel(x_prompt, x_sample, cache_k, cache_v, cache_idx_k, state_conv, c_prompt, c_sample,
           w_ada, b_ada, g_norm, w_in, w_a, w_dw, b_dw, g_ln, b_ln, w_b, w_out, g_final):
    depth = w_in.shape[0]
    assert depth == 1, "single-layer trunk"
    d = x_prompt.shape[-1]
    bp, tp = x_prompt.shape[:2]
    bs, ts = x_sample.shape[:2]
    wqkvg, widx, w2 = _split_weights(w_in[0], d)
    wts = (w_ada[0].astype(BF16), b_ada[0], g_norm[0], wqkvg, widx, w2, w_a[0].astype(BF16),
           w_dw[0], b_dw[0], g_ln[0], b_ln[0], w_b[0].astype(BF16), w_out[0].astype(BF16))
    zero_conv = jnp.zeros((bp, HALO, d), x_prompt.dtype)
    yp, kp, vp, ikp, cp = _layer(x_prompt, c_prompt, None, None, None, zero_conv, wts, g_final)
    pk = cache_k[0].reshape(bs, -1, KV_DIM).astype(BF16)
    pv = cache_v[0].reshape(bs, -1, KV_DIM).astype(BF16)
    pik = cache_idx_k[0].astype(BF16)
    ys, k_s, v_s, ik_s, c_s = _layer(x_sample, c_sample, pk, pv, pik, state_conv[0], wts, g_final)
    heads = lambda a: a.reshape(1, a.shape[0], a.shape[1], N_KV_HEADS, HEAD_DIM)
    return (yp, ys, heads(kp), heads(vp), ikp[None], cp[None],
            heads(k_s), heads(v_s), ik_s[None], c_s[None])
```

```python
import functools

import jax
import jax.numpy as jnp
from jax import lax
from jax.experimental import pallas as pl
from jax.experimental.pallas import tpu as pltpu

CHUNK = 64
CHUNK_SHIFT = CHUNK.bit_length() - 1
assert CHUNK == 1 << CHUNK_SHIFT
N_HEADS = 8
HEAD_DIM = 128
N_KV_HEADS = 2
GROUP = N_HEADS // N_KV_HEADS
ATTN_DIM = N_HEADS * HEAD_DIM
KV_DIM = N_KV_HEADS * HEAD_DIM
IDX_HEADS = 8
IDX_DIM = 64
TOPK_MAX = 256
CONV_WIDTH = 31
HALO = CONV_WIDTH - 1
EPS = 1e-6

LANES = 128
SUBLANES = 8
HALO_PAD = 32
NEG_BIG = -1e30
FLT_BIG = 3e38
LOG2E = 1.4426950408889634
VMEM_LIMIT = 56 * 1024 * 1024
MAX_BISECT = 40
ATTN_STACK = 2
assert GROUP % ATTN_STACK == 0
ATTN_KEYS = 512
BF16 = jnp.bfloat16
F32 = jnp.float32


def _silu(x):
    return x * jax.nn.sigmoid(x)


def _dot(a, b):
    return jnp.dot(a, b, preferred_element_type=F32)


def _dot_nt(a, b):
    return lax.dot_general(a, b, (((1,), (1,)), ((), ())), preferred_element_type=F32)


def _mod_kernel(c_ref, w_ref, b_ref, g_ref, gmul_ref, shift_ref, gate_ref):
    d = c_ref.shape[1]
    mod = _dot(_silu(c_ref[...]).astype(BF16), w_ref[...]) + b_ref[...]
    shift_ref[...] = mod[:, :d]
    gmul_ref[...] = g_ref[...] * (1.0 + mod[:, d:2 * d])
    gate_ref[...] = mod[:, 2 * d:]


def _modulation(c, w_ada_bf, b_ada, g_norm):
    b, d = c.shape
    out = jax.ShapeDtypeStruct((b, d), F32)
    return pl.pallas_call(
        _mod_kernel,
        name="modulation",
        out_shape=(out, out, out),
        compiler_params=pltpu.CompilerParams(vmem_limit_bytes=VMEM_LIMIT),
    )(c, w_ada_bf, b_ada.reshape(1, -1), g_norm.reshape(1, -1))


def _normed_input(x_ref, gmul_ref, shift_ref):
    x = x_ref[0]
    ms = jnp.mean(x * x, axis=-1, keepdims=True)
    return (x * lax.rsqrt(ms + EPS) * gmul_ref[0] + shift_ref[0]).astype(BF16)


def _qkv_kernel(x_ref, gmul_ref, shift_ref, wqkvg_ref, widx_ref,
                q_ref, k_ref, v_ref, kb_ref, vb_ref, ik_ref, ikb_ref, iqw_ref, iw_ref, sga_ref):
    hb = _normed_input(x_ref, gmul_ref, shift_ref)
    z = _dot(hb, wqkvg_ref[...])
    q_ref[0] = (z[:, :ATTN_DIM] * (HEAD_DIM ** -0.5 * LOG2E)).astype(BF16)
    k = z[:, ATTN_DIM:ATTN_DIM + KV_DIM]
    v = z[:, ATTN_DIM + KV_DIM:ATTN_DIM + 2 * KV_DIM]
    k_ref[0] = k
    v_ref[0] = v
    kb_ref[0] = k.astype(BF16)
    vb_ref[0] = v.astype(BF16)
    sga_ref[0] = _silu(z[:, ATTN_DIM + 2 * KV_DIM:]).astype(BF16)
    zi = _dot(hb, widx_ref[...])
    nq = IDX_HEADS * IDX_DIM
    ik = zi[:, nq:nq + IDX_DIM]
    ik_ref[0] = ik
    ikb_ref[0] = ik.astype(BF16)
    iw = zi[:, nq + IDX_DIM:nq + IDX_DIM + IDX_HEADS]
    iw_ref[0] = iw
    wscale = (IDX_HEADS ** -0.5) * (IDX_DIM ** -0.5)
    for h in range(IDX_HEADS):
        iqw_ref[0, h] = (zi[:, h * IDX_DIM:(h + 1) * IDX_DIM] * (iw[:, h:h + 1] * wscale)).astype(BF16)


def _qkv(x, gmul, shift, wqkvg, widx, tb):
    b, t, d = x.shape
    grid = (b, t // tb)
    tok = lambda n, dt: jax.ShapeDtypeStruct((b, t, n), dt)
    blk = lambda n: pl.BlockSpec((1, tb, n), lambda i, j: (i, j, 0))
    vec = pl.BlockSpec((1, 1, d), lambda i, j: (i, 0, 0))
    full = lambda a: pl.BlockSpec(a.shape, lambda i, j: (0,) * a.ndim)
    return pl.pallas_call(
        _qkv_kernel,
        name="qkv_indexer_proj",
        grid=grid,
        in_specs=[blk(d), vec, vec, full(wqkvg), full(widx)],
        out_specs=(blk(ATTN_DIM), blk(KV_DIM), blk(KV_DIM), blk(KV_DIM), blk(KV_DIM),
                   blk(IDX_DIM), blk(IDX_DIM),
                   pl.BlockSpec((1, IDX_HEADS, tb, IDX_DIM), lambda i, j: (i, 0, j, 0)),
                   blk(IDX_HEADS), blk(ATTN_DIM)),
        out_shape=(tok(ATTN_DIM, BF16), tok(KV_DIM, F32), tok(KV_DIM, F32), tok(KV_DIM, BF16),
                   tok(KV_DIM, BF16), tok(IDX_DIM, F32), tok(IDX_DIM, BF16),
                   jax.ShapeDtypeStruct((b, IDX_HEADS, t, IDX_DIM), BF16),
                   tok(IDX_HEADS, F32), tok(ATTN_DIM, BF16)),
        compiler_params=pltpu.CompilerParams(
            dimension_semantics=("arbitrary", "arbitrary"), vmem_limit_bytes=VMEM_LIMIT),
    )(x, gmul, shift, wqkvg, widx)


def _conv_kernel(x_ref, gmul_ref, shift_ref, w2_ref, cs_ref, wdw_ref, bdw_ref, gln_ref, bln_ref, wb_ref,
                 pb_ref, sma_ref, cout_ref, ubuf, shbuf):
    tb = x_ref.shape[1]
    d = x_ref.shape[2]
    lead = HALO_PAD - HALO

    @pl.when(pl.program_id(1) == 0)
    def _():
        ubuf[0:HALO_PAD, :] = jnp.zeros((HALO_PAD, d), F32)
        ubuf[lead:HALO_PAD, :] = cs_ref[0]

    hb = _normed_input(x_ref, gmul_ref, shift_ref)
    z = _dot(hb, w2_ref[...])
    u = z[:, :d] * jax.nn.sigmoid(z[:, d:2 * d])
    ubuf[HALO_PAD:HALO_PAD + tb, :] = u
    dw = jnp.zeros((tb, d), F32) + bdw_ref[...]
    for r in range(SUBLANES):
        taps = [j for j in range(CONV_WIDTH) if (lead + j) % SUBLANES == r]
        span = lead + taps[-1] - r + tb
        src = ubuf
        if r:
            shbuf[0:span, :] = ubuf[r:r + span, :]
            src = shbuf
        for j in taps:
            off = lead + j - r
            dw = dw + src[off:off + tb, :] * wdw_ref[j:j + 1, :]
    tail = ubuf[tb + lead:tb + HALO_PAD, :]
    cout_ref[0] = tail
    ubuf[lead:HALO_PAD, :] = tail
    mu = jnp.mean(dw, axis=-1, keepdims=True)
    cen = dw - mu
    var = jnp.mean(cen * cen, axis=-1, keepdims=True)
    n = cen * lax.rsqrt(var + EPS) * gln_ref[...] + bln_ref[...]
    yb = _dot((_silu(n) * _silu(z[:, 2 * d:3 * d])).astype(BF16), wb_ref[...])
    sma_ref[0] = jax.nn.sigmoid(z[:, 3 * d:4 * d]).astype(BF16)
    pb_ref[0] = (jax.nn.sigmoid(z[:, 4 * d:5 * d]) * yb).astype(BF16)


def _conv(x, gmul, shift, w2, conv_state, w_dw, b_dw, g_ln, b_ln, wb, tb):
    b, t, d = x.shape
    grid = (b, t // tb)
    blk = pl.BlockSpec((1, tb, d), lambda i, j: (i, j, 0))
    vec = pl.BlockSpec((1, 1, d), lambda i, j: (i, 0, 0))
    halo = pl.BlockSpec((1, HALO, d), lambda i, j: (i, 0, 0))
    full = lambda a: pl.BlockSpec(a.shape, lambda i, j: (0,) * a.ndim)
    row = lambda a: a.reshape(1, -1)
    args = (x, gmul, shift, w2, conv_state, w_dw, row(b_dw), row(g_ln), row(b_ln), wb)
    return pl.pallas_call(
        _conv_kernel,
        name="conv_branch",
        grid=grid,
        in_specs=[blk, vec, vec, full(w2), halo] + [full(a) for a in args[5:]],
        out_specs=(blk, blk, halo),
        out_shape=(jax.ShapeDtypeStruct((b, t, d), BF16), jax.ShapeDtypeStruct((b, t, d), BF16),
                   jax.ShapeDtypeStruct((b, HALO, d), F32)),
        scratch_shapes=[pltpu.VMEM((tb + HALO_PAD, d), F32), pltpu.VMEM((tb + HALO_PAD, d), F32)],
        compiler_params=pltpu.CompilerParams(
            dimension_semantics=("arbitrary", "arbitrary"), vmem_limit_bytes=VMEM_LIMIT),
    )(*args)


def _lane_partial(x, op):
    acc = x[:, :LANES]
    for c in range(1, x.shape[1] // LANES):
        acc = op(acc, x[:, c * LANES:(c + 1) * LANES])
    return acc


def _attend_kernel(x_ref, gate_ref, gfin_ref, q_ref, iqw_ref, iw_ref, sga_ref, sma_ref, pb_ref,
                   kt_ref, vb_ref, ikb_ref, wa_ref, wout_ref, y_ref, s_ref, bias_ref,
                   *, past, length, topk, lb):
    qb = x_ref.shape[1]
    i = pl.program_id(1)
    qpos0 = past + i * qb
    kf = jnp.float32(topk)

    def visible(qpos):
        return jnp.minimum(length, (jnp.right_shift(qpos, CHUNK_SHIFT) + 1) * CHUNK)

    nfull = lax.div(visible(qpos0), jnp.int32(lb))
    nkb = lax.div(visible(qpos0 + qb - 1) + lb - 1, jnp.int32(lb))
    nadm = visible(qpos0 + lax.broadcasted_iota(jnp.int32, (qb, 1), 0)).astype(F32)

    iw = iw_ref[0]
    lo_b, hi_b, iqs = [], [], []
    for h in range(IDX_HEADS):
        wpos = iw[:, h:h + 1] >= 0.0
        lo_b.append(jnp.where(wpos, 0.0, -jnp.inf))
        hi_b.append(jnp.where(wpos, jnp.inf, 0.0))
        iqs.append(iqw_ref[0, h])

    def score(kb):
        ikblk = ikb_ref[0, pl.ds(pl.multiple_of(kb * lb, lb), lb), :]
        acc = jnp.minimum(jnp.maximum(_dot_nt(iqs[0], ikblk), lo_b[0]), hi_b[0])
        for h in range(1, IDX_HEADS):
            acc = acc + jnp.minimum(jnp.maximum(_dot_nt(iqs[h], ikblk), lo_b[h]), hi_b[h])
        return acc

    def whole_block(kb, carry):
        rmin, rmax = carry
        acc = score(kb)
        s_ref[kb] = acc
        return (jnp.minimum(rmin, _lane_partial(acc, jnp.minimum)),
                jnp.maximum(rmax, _lane_partial(acc, jnp.maximum)))

    def edge_block(kb, carry):
        rmin, rmax = carry
        acc = score(kb)
        kpos = kb * lb + lax.broadcasted_iota(jnp.int32, (qb, lb), 1)
        adm = kpos.astype(F32) < nadm
        s_ref[kb] = jnp.where(adm, acc, -jnp.inf)
        return (jnp.minimum(rmin, _lane_partial(jnp.where(adm, acc, jnp.inf), jnp.minimum)),
                jnp.maximum(rmax, _lane_partial(jnp.where(adm, acc, -jnp.inf), jnp.maximum)))

    ext = (jnp.full((qb, LANES), jnp.inf, F32), jnp.full((qb, LANES), -jnp.inf, F32))
    ext = lax.fori_loop(0, jnp.right_shift(nfull, 1),
                        lambda j, c: whole_block(2 * j + 1, whole_block(2 * j, c)), ext)
    ext = lax.cond(jnp.bitwise_and(nfull, 1) == 1, lambda c: whole_block(nfull - 1, c), lambda c: c, ext)
    rmin, rmax = lax.fori_loop(nfull, nkb, edge_block, ext)
    rmin = jnp.min(rmin, axis=-1, keepdims=True)
    rmax = jnp.max(rmax, axis=-1, keepdims=True)

    def fold(pick, op, start):
        def body(kb, acc):
            for c in range(lb // LANES):
                acc = op(acc, pick(s_ref[kb, :, c * LANES:(c + 1) * LANES]))
            return acc
        return lax.fori_loop(0, nkb, body, jnp.full((qb, LANES), start, F32))

    def count(pred):
        acc = fold(lambda s: jnp.where(pred(s), 1.0, 0.0), jnp.add, 0.0)
        return jnp.sum(acc, axis=-1, keepdims=True)

    def any_row(flag):
        return jnp.max(jnp.where(flag, 1.0, 0.0)) > 0.0

    cge0 = count(lambda s: s >= 0.0)
    cgt0 = count(lambda s: s > 0.0)
    take_all = nadm <= kf
    tie0 = jnp.logical_not(take_all) & (cgt0 < kf) & (cge0 >= kf)
    pos = cgt0 >= kf
    lo = jnp.where(pos, 0.0, rmin)
    hi = jnp.where(pos, rmax, 0.0)
    done = take_all | tie0 | (jnp.where(pos, cge0, nadm) == kf)

    def bisect_cond(st):
        it, lo, hi, open_ = st
        return (it < MAX_BISECT) & (jnp.max(open_) > 0.0)

    def bisect_body(st):
        it, lo, hi, open_ = st
        mid = lo + (hi - lo) * 0.5
        c = count(lambda s: s >= mid)
        up = c >= kf
        lo = jnp.where(up, mid, lo)
        hi = jnp.where(up, hi, mid)
        settle = c == kf
        return it + 1, lo, jnp.where(settle, lo, hi), jnp.where(settle, 0.0, open_)

    _, lo, hi, open_ = lax.while_loop(bisect_cond, bisect_body,
                                      (jnp.int32(0), lo, jnp.where(done, lo, hi), jnp.where(done, 0.0, 1.0)))
    done = open_ <= 0.0
    thr = jnp.where(take_all, -FLT_BIG, jnp.where(tie0, 0.0, lo))
    ties = jnp.where(tie0, kf - cgt0, jnp.inf)

    def snap_threshold():
        def snap_cond(st):
            a, b = st
            return any_row(a < b)

        def snap_body(st):
            a, b = st
            mid = a + (b - a) * 0.5
            mid = jnp.where(mid < b, mid, a)
            more = count(lambda s: s > mid) >= kf
            above = jnp.min(fold(lambda s: jnp.where(s > mid, s, jnp.inf), jnp.minimum, jnp.inf),
                            axis=-1, keepdims=True)
            below = jnp.max(fold(lambda s: jnp.where(s <= mid, s, -jnp.inf), jnp.maximum, -jnp.inf),
                            axis=-1, keepdims=True)
            return jnp.where(more & (a < b), above, a), jnp.where(more | (a >= b), b, below)

        a0 = jnp.min(fold(lambda s: jnp.where(s >= lo, s, jnp.inf), jnp.minimum, jnp.inf),
                     axis=-1, keepdims=True)
        b0 = jnp.max(fold(lambda s: jnp.where(s <= hi, s, -jnp.inf), jnp.maximum, -jnp.inf),
                     axis=-1, keepdims=True)
        a, _ = lax.while_loop(snap_cond, snap_body, (jnp.where(done, thr, a0), jnp.where(done, thr, b0)))
        want = kf - count(lambda s: s > a)
        return jnp.where(done, thr, a), jnp.where(done, ties, want)

    thr, ties = lax.cond(any_row(jnp.logical_not(done)), snap_threshold, lambda: (thr, ties))

    def mask_with_ties():
        r_i = lax.broadcasted_iota(jnp.int32, (LANES, LANES), 0)
        c_i = lax.broadcasted_iota(jnp.int32, (LANES, LANES), 1)
        tri = jnp.where(r_i <= c_i, 1.0, 0.0).astype(BF16)

        def body(kb, seen):
            s = s_ref[kb]
            cols = []
            for c in range(lb // LANES):
                sc = s[:, c * LANES:(c + 1) * LANES]
                eq = sc == thr
                eq_f = jnp.where(eq, 1.0, 0.0)
                before = seen + _dot(eq_f.astype(BF16), tri) - eq_f
                keep = (sc > thr) | (eq & (before < ties))
                cols.append(jnp.where(keep, 0.0, NEG_BIG).astype(BF16))
                seen = seen + jnp.sum(eq_f, axis=-1, keepdims=True)
            bias_ref[kb] = jnp.concatenate(cols, axis=1)
            return seen

        lax.fori_loop(0, nkb, body, jnp.zeros((qb, 1), F32))

    def mask_plain():
        def body(kb, carry):
            bias_ref[kb] = jnp.where(s_ref[kb] >= thr, 0.0, NEG_BIG).astype(BF16)
            return carry

        lax.fori_loop(0, nkb, body, jnp.int32(0))

    lax.cond(any_row(ties < jnp.inf), mask_with_ties, mask_plain)

    q = q_ref[0]
    eye = jnp.where(lax.broadcasted_iota(jnp.int32, (qb, qb), 0) == lax.broadcasted_iota(jnp.int32, (qb, qb), 1),
                    1.0, 0.0).astype(BF16)
    ones = jnp.ones((ATTN_KEYS, HEAD_DIM), BF16)
    stacks = [(h0 // GROUP, h0) for h0 in range(0, N_HEADS, ATTN_STACK)]
    lhs = [jnp.concatenate(
        [jnp.concatenate([q[:, h * HEAD_DIM:(h + 1) * HEAD_DIM], eye], axis=1)
         for h in range(h0, h0 + ATTN_STACK)], axis=0) for _, h0 in stacks]

    def attn_part(kb, part, carry):
        c0 = part * ATTN_KEYS
        bias = bias_ref[kb, :, c0:c0 + ATTN_KEYS]
        vrows = vb_ref[0, pl.ds(pl.multiple_of(kb * lb, lb) + c0, ATTN_KEYS), :]
        rhs = [jnp.concatenate([kt_ref[0, kb, g * HEAD_DIM:(g + 1) * HEAD_DIM, c0:c0 + ATTN_KEYS], bias], axis=0)
               for g in range(N_KV_HEADS)]
        vext = [jnp.concatenate([vrows[:, g * HEAD_DIM:(g + 1) * HEAD_DIM], ones], axis=1)
                for g in range(N_KV_HEADS)]
        logits = [_dot(lhs_c, rhs[g]) for (g, _), lhs_c in zip(stacks, lhs)]
        m_new = [jnp.maximum(m, jnp.max(s, axis=-1, keepdims=True)) for s, (m, _) in zip(logits, carry)]
        probs = [jnp.exp2(s - mn).astype(BF16) for s, mn in zip(logits, m_new)]
        pv = [_dot(p, vext[g]) for (g, _), p in zip(stacks, probs)]
        return tuple((mn, jnp.exp2(m - mn) * acc + o) for mn, (m, acc), o in zip(m_new, carry, pv))

    def attn_block(kb, carry):
        for part in range(lb // ATTN_KEYS):
            carry = attn_part(kb, part, carry)
        return carry

    def attn_pair(j, carry):
        return attn_block(2 * j + 1, attn_block(2 * j, carry))

    init = tuple((jnp.full((ATTN_STACK * qb, 1), NEG_BIG, F32),
                  jnp.zeros((ATTN_STACK * qb, 2 * HEAD_DIM), F32)) for _ in stacks)
    res = lax.fori_loop(0, jnp.right_shift(nkb, 1), attn_pair, init)
    res = lax.cond(jnp.bitwise_and(nkb, 1) == 1, lambda c: attn_block(nkb - 1, c), lambda c: c, res)
    outs = []
    for _, acc in res:
        o = acc[:, :HEAD_DIM] / acc[:, HEAD_DIM:HEAD_DIM + 1]
        outs.extend(o[r * qb:(r + 1) * qb] for r in range(ATTN_STACK))
    o_a = jnp.concatenate(outs, axis=-1)

    ya = _dot((o_a * sga_ref[0].astype(F32)).astype(BF16), wa_ref[...])
    mixed = sma_ref[0].astype(F32) * ya + pb_ref[0].astype(F32)
    xo = x_ref[0] + gate_ref[0] * _dot(mixed.astype(BF16), wout_ref[...])
    ms = jnp.mean(xo * xo, axis=-1, keepdims=True)
    y_ref[0] = xo * lax.rsqrt(ms + EPS) * gfin_ref[...]


def _attend(x, gate, g_final, q, iqw, iw, sga, sma, pb, kall, vall, ikall, wa, wout,
            *, past, length, qb, lb):
    b, t, d = x.shape
    lp = kall.shape[1]
    nkb = lp // lb
    topk = min(TOPK_MAX, length // 4)
    grid = (b, t // qb)
    kt = kall.reshape(b, nkb, lb, KV_DIM).transpose(0, 1, 3, 2)
    blk = lambda n: pl.BlockSpec((1, qb, n), lambda i, j: (i, j, 0))
    vec = pl.BlockSpec((1, 1, d), lambda i, j: (i, 0, 0))
    keys = lambda n: pl.BlockSpec((1, lp, n), lambda i, j: (i, 0, 0))
    full = lambda a: pl.BlockSpec(a.shape, lambda i, j: (0,) * a.ndim)
    gfin = g_final.reshape(1, -1)
    kern = functools.partial(_attend_kernel, past=past, length=length, topk=topk, lb=lb)
    return pl.pallas_call(
        kern,
        name="sparse_attend_out",
        grid=grid,
        in_specs=[blk(d), vec, full(gfin), blk(ATTN_DIM),
                  pl.BlockSpec((1, IDX_HEADS, qb, IDX_DIM), lambda i, j: (i, 0, j, 0)),
                  blk(IDX_HEADS), blk(ATTN_DIM), blk(d), blk(d),
                  pl.BlockSpec((1, nkb, KV_DIM, lb), lambda i, j: (i, 0, 0, 0)),
                  keys(KV_DIM), keys(IDX_DIM), full(wa), full(wout)],
        out_specs=blk(d),
        out_shape=jax.ShapeDtypeStruct((b, t, d), F32),
        scratch_shapes=[pltpu.VMEM((nkb, qb, lb), F32), pltpu.VMEM((nkb, qb, lb), BF16)],
        compiler_params=pltpu.CompilerParams(
            dimension_semantics=("arbitrary", "arbitrary"), vmem_limit_bytes=VMEM_LIMIT),
    )(x, gate, gfin, q, iqw, iw, sga, sma, pb, kt, vall, ikall, wa, wout)


def _block_sizes(t):
    tb = 256 if t % 256 == 0 else t
    qb = 128 if t % 128 == 0 else t
    return tb, qb, 512


def _split_weights(w_in, d):
    o_iq = 2 * ATTN_DIM + 2 * KV_DIM
    o_iw = o_iq + IDX_HEADS * IDX_DIM
    o_ik = o_iw + IDX_HEADS
    o_glu = o_ik + IDX_DIM
    nidx = IDX_HEADS * IDX_DIM + IDX_DIM + IDX_HEADS
    pad = (-nidx) % LANES
    wqkvg = w_in[:, :o_iq].astype(BF16)
    widx = jnp.concatenate([w_in[:, o_iq:o_iw], w_in[:, o_ik:o_glu], w_in[:, o_iw:o_ik],
                            jnp.zeros((d, pad), w_in.dtype)], axis=1).astype(BF16)
    w2 = w_in[:, o_glu:].astype(BF16)
    return wqkvg, widx, w2


def _layer(x, c, past_k, past_v, past_ik, conv_state, wts, g_final):
    w_ada_bf, b_ada, g_norm, wqkvg, widx, w2, wa, w_dw, b_dw, g_ln, b_ln, wb, wout = wts
    b, t, d = x.shape
    assert t >= HALO and t % 8 == 0
    tb, qb, lb = _block_sizes(t)
    gmul, shift, gate = (m.reshape(b, 1, d) for m in _modulation(c, w_ada_bf, b_ada, g_norm))
    q, k, v, kb, vb, ik, ikb, iqw, iw, sga = _qkv(x, gmul, shift, wqkvg, widx, tb)
    pb, sma, new_conv = _conv(x, gmul, shift, w2, conv_state, w_dw, b_dw, g_ln, b_ln, wb, tb)
    past = 0 if past_k is None else past_k.shape[1]
    length = past + t
    if past:
        kb = jnp.concatenate([past_k, kb], axis=1)
        vb = jnp.concatenate([past_v, vb], axis=1)
        ikb = jnp.concatenate([past_ik, ikb], axis=1)
    pad = (-length) % lb
    if pad:
        kb, vb, ikb = (jnp.pad(a, ((0, 0), (0, pad), (0, 0))) for a in (kb, vb, ikb))
    y = _attend(x, gate, g_final, q, iqw, iw, sga, sma, pb, kb, vb, ikb, wa, wout,
                past=past, length=length, qb=qb, lb=lb)
    return y, k, v, ik, new_conv


def kernel(x_prompt, x_sample, cache_k, cache_v, cache_idx_k, state_conv, c_prompt, c_sample,
           w_ada, b_ada, g_norm, w_in, w_a, w_dw, b_dw, g_ln, b_ln, w_b, w_out, g_final):
    depth = w_in.shape[0]
    assert depth == 1, "single-layer trunk"
    d = x_prompt.shape[-1]
    bp, tp = x_prompt.shape[:2]
    bs, ts = x_sample.shape[:2]
    wqkvg, widx, w2 = _split_weights(w_in[0], d)
    wts = (w_ada[0].astype(BF16), b_ada[0], g_norm[0], wqkvg, widx, w2, w_a[0].astype(BF16),
           w_dw[0], b_dw[0], g_ln[0], b_ln[0], w_b[0].astype(BF16), w_out[0].astype(BF16))
    zero_conv = jnp.zeros((bp, HALO, d), x_prompt.dtype)
    yp, kp, vp, ikp, cp = _layer(x_prompt, c_prompt, None, None, None, zero_conv, wts, g_final)
    pk = cache_k[0].reshape(bs, -1, KV_DIM).astype(BF16)
    pv = cache_v[0].reshape(bs, -1, KV_DIM).astype(BF16)
    pik = cache_idx_k[0].astype(BF16)
    ys, k_s, v_s, ik_s, c_s = _layer(x_sample, c_sample, pk, pv, pik, state_conv[0], wts, g_final)
    heads = lambda a: a.reshape(1, a.shape[0], a.shape[1], N_KV_HEADS, HEAD_DIM)
    return (yp, ys, heads(kp), heads(vp), ikp[None], cp[None],
            heads(k_s), heads(v_s), ik_s[None], c_s[None])
```

```python
import functools

import jax
import jax.numpy as jnp
from jax import lax
from jax.experimental import pallas as pl
from jax.experimental.pallas import tpu as pltpu

CHUNK = 64
CHUNK_SHIFT = CHUNK.bit_length() - 1
assert CHUNK == 1 << CHUNK_SHIFT
N_HEADS = 8
HEAD_DIM = 128
N_KV_HEADS = 2
GROUP = N_HEADS // N_KV_HEADS
ATTN_DIM = N_HEADS * HEAD_DIM
KV_DIM = N_KV_HEADS * HEAD_DIM
IDX_HEADS = 8
IDX_DIM = 64
TOPK_MAX = 256
CONV_WIDTH = 31
HALO = CONV_WIDTH - 1
EPS = 1e-6

LANES = 128
SUBLANES = 8
HALO_PAD = 32
NEG_BIG = -1e30
FLT_BIG = 3e38
LOG2E = 1.4426950408889634
VMEM_LIMIT = 56 * 1024 * 1024
MAX_BISECT = 40
BISECT_UNROLL = 2
assert MAX_BISECT % BISECT_UNROLL == 0
ATTN_STACK = 2
assert GROUP % ATTN_STACK == 0

BF16 = jnp.bfloat16
F32 = jnp.float32


def _silu(x):
    return x * jax.nn.sigmoid(x)


def _dot(a, b):
    return jnp.dot(a, b, preferred_element_type=F32)


def _dot_nt(a, b):
    return lax.dot_general(a, b, (((1,), (1,)), ((), ())), preferred_element_type=F32)


def _mod_kernel(c_ref, w_ref, b_ref, g_ref, gmul_ref, shift_ref, gate_ref):
    d = c_ref.shape[1]
    mod = _dot(_silu(c_ref[...]).astype(BF16), w_ref[...]) + b_ref[...]
    shift_ref[...] = mod[:, :d]
    gmul_ref[...] = g_ref[...] * (1.0 + mod[:, d:2 * d])
    gate_ref[...] = mod[:, 2 * d:]


def _modulation(c, w_ada_bf, b_ada, g_norm):
    b, d = c.shape
    out = jax.ShapeDtypeStruct((b, d), F32)
    return pl.pallas_call(
        _mod_kernel,
        name="modulation",
        out_shape=(out, out, out),
        compiler_params=pltpu.CompilerParams(vmem_limit_bytes=VMEM_LIMIT),
    )(c, w_ada_bf, b_ada.reshape(1, -1), g_norm.reshape(1, -1))


def _normed_input(x_ref, gmul_ref, shift_ref):
    x = x_ref[0]
    ms = jnp.mean(x * x, axis=-1, keepdims=True)
    return (x * lax.rsqrt(ms + EPS) * gmul_ref[0] + shift_ref[0]).astype(BF16)


def _qkv_kernel(x_ref, gmul_ref, shift_ref, wqkvg_ref, widx_ref,
                q_ref, k_ref, v_ref, kb_ref, vb_ref, ik_ref, ikb_ref, iqw_ref, ikw_ref, sga_ref):
    hb = _normed_input(x_ref, gmul_ref, shift_ref)
    z = _dot(hb, wqkvg_ref[...])
    q_ref[0] = (z[:, :ATTN_DIM] * (HEAD_DIM ** -0.5 * LOG2E)).astype(BF16)
    k = z[:, ATTN_DIM:ATTN_DIM + KV_DIM]
    v = z[:, ATTN_DIM + KV_DIM:ATTN_DIM + 2 * KV_DIM]
    k_ref[0] = k
    v_ref[0] = v
    kb_ref[0] = k.astype(BF16)
    vb_ref[0] = v.astype(BF16)
    sga_ref[0] = _silu(z[:, ATTN_DIM + 2 * KV_DIM:]).astype(BF16)
    zi = _dot(hb, widx_ref[...])
    nq = IDX_HEADS * IDX_DIM
    ik = zi[:, nq:nq + IDX_DIM]
    ik_ref[0] = ik
    ikb_ref[0] = ik.astype(BF16)
    iw = zi[:, nq + IDX_DIM:nq + IDX_DIM + IDX_HEADS]
    ikw_ref[0] = zi[:, nq:nq + LANES]
    wscale = (IDX_HEADS ** -0.5) * (IDX_DIM ** -0.5)
    for h in range(IDX_HEADS):
        iqw_ref[0, h] = (zi[:, h * IDX_DIM:(h + 1) * IDX_DIM] * (iw[:, h:h + 1] * wscale)).astype(BF16)


def _qkv(x, gmul, shift, wqkvg, widx, tb):
    b, t, d = x.shape
    grid = (b, t // tb)
    tok = lambda n, dt: jax.ShapeDtypeStruct((b, t, n), dt)
    blk = lambda n: pl.BlockSpec((1, tb, n), lambda i, j: (i, j, 0))
    vec = pl.BlockSpec((1, 1, d), lambda i, j: (i, 0, 0))
    full = lambda a: pl.BlockSpec(a.shape, lambda i, j: (0,) * a.ndim)
    return pl.pallas_call(
        _qkv_kernel,
        name="qkv_indexer_proj",
        grid=grid,
        in_specs=[blk(d), vec, vec, full(wqkvg), full(widx)],
        out_specs=(blk(ATTN_DIM), blk(KV_DIM), blk(KV_DIM), blk(KV_DIM), blk(KV_DIM),
                   blk(IDX_DIM), blk(IDX_DIM),
                   pl.BlockSpec((1, IDX_HEADS, tb, IDX_DIM), lambda i, j: (i, 0, j, 0)),
                   blk(LANES), blk(ATTN_DIM)),
        out_shape=(tok(ATTN_DIM, BF16), tok(KV_DIM, F32), tok(KV_DIM, F32), tok(KV_DIM, BF16),
                   tok(KV_DIM, BF16), tok(IDX_DIM, F32), tok(IDX_DIM, BF16),
                   jax.ShapeDtypeStruct((b, IDX_HEADS, t, IDX_DIM), BF16),
                   tok(LANES, F32), tok(ATTN_DIM, BF16)),
        compiler_params=pltpu.CompilerParams(
            dimension_semantics=("arbitrary", "arbitrary"), vmem_limit_bytes=VMEM_LIMIT),
    )(x, gmul, shift, wqkvg, widx)


def _conv_kernel(x_ref, gmul_ref, shift_ref, w2_ref, cs_ref, wdw_ref, bdw_ref, gln_ref, bln_ref, wb_ref,
                 pb_ref, sma_ref, cout_ref, ubuf, shbuf):
    tb = x_ref.shape[1]
    d = x_ref.shape[2]
    lead = HALO_PAD - HALO

    @pl.when(pl.program_id(1) == 0)
    def _():
        ubuf[0:HALO_PAD, :] = jnp.zeros((HALO_PAD, d), F32)
        ubuf[lead:HALO_PAD, :] = cs_ref[0]

    hb = _normed_input(x_ref, gmul_ref, shift_ref)
    z = _dot(hb, w2_ref[...])
    u = z[:, :d] * jax.nn.sigmoid(z[:, d:2 * d])
    ubuf[HALO_PAD:HALO_PAD + tb, :] = u
    dw = jnp.zeros((tb, d), F32) + bdw_ref[...]
    for r in range(SUBLANES):
        taps = [j for j in range(CONV_WIDTH) if (lead + j) % SUBLANES == r]
        span = lead + taps[-1] - r + tb
        src = ubuf
        if r:
            shbuf[0:span, :] = ubuf[r:r + span, :]
            src = shbuf
        for j in taps:
            off = lead + j - r
            dw = dw + src[off:off + tb, :] * wdw_ref[j:j + 1, :]
    tail = ubuf[tb + lead:tb + HALO_PAD, :]
    cout_ref[0] = tail
    ubuf[lead:HALO_PAD, :] = tail
    mu = jnp.mean(dw, axis=-1, keepdims=True)
    cen = dw - mu
    var = jnp.mean(cen * cen, axis=-1, keepdims=True)
    n = cen * lax.rsqrt(var + EPS) * gln_ref[...] + bln_ref[...]
    yb = _dot((_silu(n) * _silu(z[:, 2 * d:3 * d])).astype(BF16), wb_ref[...])
    sma_ref[0] = jax.nn.sigmoid(z[:, 3 * d:4 * d]).astype(BF16)
    pb_ref[0] = (jax.nn.sigmoid(z[:, 4 * d:5 * d]) * yb).astype(BF16)


def _conv(x, gmul, shift, w2, conv_state, w_dw, b_dw, g_ln, b_ln, wb, tb):
    b, t, d = x.shape
    grid = (b, t // tb)
    blk = pl.BlockSpec((1, tb, d), lambda i, j: (i, j, 0))
    vec = pl.BlockSpec((1, 1, d), lambda i, j: (i, 0, 0))
    halo = pl.BlockSpec((1, HALO, d), lambda i, j: (i, 0, 0))
    full = lambda a: pl.BlockSpec(a.shape, lambda i, j: (0,) * a.ndim)
    row = lambda a: a.reshape(1, -1)
    args = (x, gmul, shift, w2, conv_state, w_dw, row(b_dw), row(g_ln), row(b_ln), wb)
    return pl.pallas_call(
        _conv_kernel,
        name="conv_branch",
        grid=grid,
        in_specs=[blk, vec, vec, full(w2), halo] + [full(a) for a in args[5:]],
        out_specs=(blk, blk, halo),
        out_shape=(jax.ShapeDtypeStruct((b, t, d), BF16), jax.ShapeDtypeStruct((b, t, d), BF16),
                   jax.ShapeDtypeStruct((b, HALO, d), F32)),
        scratch_shapes=[pltpu.VMEM((tb + HALO_PAD, d), F32), pltpu.VMEM((tb + HALO_PAD, d), F32)],
        compiler_params=pltpu.CompilerParams(
            dimension_semantics=("arbitrary", "arbitrary"), vmem_limit_bytes=VMEM_LIMIT),
    )(*args)


def _attend_kernel(x_ref, gate_ref, gfin_ref, q_ref, iqw_ref, ikw_ref, sga_ref, sma_ref, pb_ref,
                   kb_ref, vb_ref, ikb_ref, wa_ref, wout_ref, y_ref, st_ref, bias_ref,
                   *, past, length, topk, lb):
    qb = x_ref.shape[1]
    i = pl.program_id(1)
    qpos0 = past + i * qb
    kf = jnp.float32(topk)
    row_chunk = min(lb, LANES)

    def visible(qpos):
        return jnp.minimum(length, (jnp.right_shift(qpos, CHUNK_SHIFT) + 1) * CHUNK)

    nfull = lax.div(visible(qpos0), jnp.int32(lb))
    nkb = lax.div(visible(qpos0 + qb - 1) + lb - 1, jnp.int32(lb))
    nadm = visible(qpos0 + lax.broadcasted_iota(jnp.int32, (1, qb), 1)).astype(F32)

    slab = ikw_ref[0]
    if qb < LANES:
        slab = jnp.concatenate([slab, jnp.zeros((LANES - qb, LANES), F32)], axis=0)
    iw_t = jnp.transpose(slab)[IDX_DIM:IDX_DIM + IDX_HEADS, :qb]
    lo_b = [jnp.where(iw_t[h:h + 1] >= 0.0, 0.0, -jnp.inf) for h in range(IDX_HEADS)]
    hi_b = [jnp.where(iw_t[h:h + 1] >= 0.0, jnp.inf, 0.0) for h in range(IDX_HEADS)]
    iq_all = iqw_ref[0].reshape(IDX_HEADS * qb, IDX_DIM)

    def score(kb):
        ikblk = ikb_ref[0, pl.ds(pl.multiple_of(kb * lb, lb), lb), :]
        dots = _dot_nt(ikblk, iq_all)
        acc = jnp.minimum(jnp.maximum(dots[:, :qb], lo_b[0]), hi_b[0])
        for h in range(1, IDX_HEADS):
            acc = acc + jnp.minimum(jnp.maximum(dots[:, h * qb:(h + 1) * qb], lo_b[h]), hi_b[h])
        return acc

    def rows(x, red):
        return red(x.reshape(x.shape[0] // SUBLANES, SUBLANES, qb), axis=0)

    def whole_block(kb, carry):
        rmin, rmax = carry
        acc = score(kb)
        st_ref[kb] = acc
        return jnp.minimum(rmin, rows(acc, jnp.min)), jnp.maximum(rmax, rows(acc, jnp.max))

    def edge_block(kb, carry):
        rmin, rmax = carry
        acc = score(kb)
        kpos = kb * lb + lax.broadcasted_iota(jnp.int32, (lb, qb), 0)
        adm = kpos.astype(F32) < nadm
        st_ref[kb] = jnp.where(adm, acc, -jnp.inf)
        return (jnp.minimum(rmin, rows(jnp.where(adm, acc, jnp.inf), jnp.min)),
                jnp.maximum(rmax, rows(jnp.where(adm, acc, -jnp.inf), jnp.max)))

    ext = (jnp.full((SUBLANES, qb), jnp.inf, F32), jnp.full((SUBLANES, qb), -jnp.inf, F32))
    ext = lax.fori_loop(0, jnp.right_shift(nfull, 1),
                        lambda j, c: whole_block(2 * j + 1, whole_block(2 * j, c)), ext)
    ext = lax.cond(jnp.bitwise_and(nfull, 1) == 1, lambda c: whole_block(nfull - 1, c), lambda c: c, ext)
    rmin, rmax = lax.fori_loop(nfull, nkb, edge_block, ext)
    rmin = jnp.min(rmin, axis=0, keepdims=True)
    rmax = jnp.max(rmax, axis=0, keepdims=True)

    def fold(pick, red, op, start):
        def body(kb, acc):
            for c in range(0, lb, row_chunk):
                acc = op(acc, rows(pick(st_ref[kb, c:c + row_chunk, :]), red))
            return acc
        return lax.fori_loop(0, nkb, body, jnp.full((SUBLANES, qb), start, F32))

    def count(pred):
        return jnp.sum(fold(lambda s: jnp.where(pred(s), 1.0, 0.0), jnp.sum, jnp.add, 0.0),
                       axis=0, keepdims=True)

    def lowest(pick):
        return jnp.min(fold(pick, jnp.min, jnp.minimum, jnp.inf), axis=0, keepdims=True)

    def highest(pick):
        return jnp.max(fold(pick, jnp.max, jnp.maximum, -jnp.inf), axis=0, keepdims=True)

    def any_query(flag):
        return jnp.max(jnp.where(flag, 1.0, 0.0)) > 0.0

    cge0 = count(lambda s: s >= 0.0)
    cgt0 = count(lambda s: s > 0.0)
    take_all = nadm <= kf
    tie0 = jnp.logical_not(take_all) & (cgt0 < kf) & (cge0 >= kf)
    pos = cgt0 >= kf
    lo = jnp.where(pos, 0.0, rmin)
    hi = jnp.where(pos, rmax, 0.0)
    done = take_all | tie0 | (jnp.where(pos, cge0, nadm) == kf)

    def bisect_cond(st):
        it, lo, hi, open_ = st
        return (it < MAX_BISECT) & (jnp.max(open_) > 0.0)

    def bisect_step(lo, hi, open_):
        mid = lo + (hi - lo) * 0.5
        c = count(lambda s: s >= mid)
        up = c >= kf
        lo = jnp.where(up, mid, lo)
        hi = jnp.where(up, hi, mid)
        settle = c == kf
        return lo, jnp.where(settle, lo, hi), jnp.where(settle, 0.0, open_)

    def bisect_body(st):
        it, lo, hi, open_ = st
        for _ in range(BISECT_UNROLL):
            lo, hi, open_ = bisect_step(lo, hi, open_)
        return it + BISECT_UNROLL, lo, hi, open_

    _, lo, hi, open_ = lax.while_loop(bisect_cond, bisect_body,
                                      (jnp.int32(0), lo, jnp.where(done, lo, hi), jnp.where(done, 0.0, 1.0)))
    done = open_ <= 0.0
    thr = jnp.where(take_all, -FLT_BIG, jnp.where(tie0, 0.0, lo))
    ties = jnp.where(tie0, kf - cgt0, jnp.inf)

    def snap_threshold():
        def snap_cond(st):
            a, b = st
            return any_query(a < b)

        def snap_body(st):
            a, b = st
            mid = a + (b - a) * 0.5
            mid = jnp.where(mid < b, mid, a)
            more = count(lambda s: s > mid) >= kf
            above = lowest(lambda s: jnp.where(s > mid, s, jnp.inf))
            below = highest(lambda s: jnp.where(s <= mid, s, -jnp.inf))
            return jnp.where(more & (a < b), above, a), jnp.where(more | (a >= b), b, below)

        a0 = lowest(lambda s: jnp.where(s >= lo, s, jnp.inf))
        b0 = highest(lambda s: jnp.where(s <= hi, s, -jnp.inf))
        a, _ = lax.while_loop(snap_cond, snap_body, (jnp.where(done, thr, a0), jnp.where(done, thr, b0)))
        want = kf - count(lambda s: s > a)
        return jnp.where(done, thr, a), jnp.where(done, ties, want)

    thr, ties = lax.cond(any_query(jnp.logical_not(done)), snap_threshold, lambda: (thr, ties))

    def mask_with_ties():
        r_i = lax.broadcasted_iota(jnp.int32, (row_chunk, row_chunk), 0)
        c_i = lax.broadcasted_iota(jnp.int32, (row_chunk, row_chunk), 1)
        tri = jnp.where(c_i <= r_i, 1.0, 0.0).astype(BF16)

        def body(kb, seen):
            for c in range(0, lb, row_chunk):
                sc = st_ref[kb, c:c + row_chunk, :]
                eq = sc == thr
                eq_f = jnp.where(eq, 1.0, 0.0)
                before = seen + _dot(tri, eq_f.astype(BF16)) - eq_f
                keep = (sc > thr) | (eq & (before < ties))
                bias_ref[kb, c:c + row_chunk, :] = jnp.where(keep, 0.0, NEG_BIG).astype(BF16)
                seen = seen + jnp.sum(eq_f, axis=0, keepdims=True)
            return seen

        lax.fori_loop(0, nkb, body, jnp.zeros((1, qb), F32))

    def mask_plain():
        def body(kb, carry):
            bias_ref[kb] = jnp.where(st_ref[kb] >= thr, 0.0, NEG_BIG).astype(BF16)
            return carry

        lax.fori_loop(0, nkb, body, jnp.int32(0))

    lax.cond(any_query(ties < jnp.inf), mask_with_ties, mask_plain)

    q = q_ref[0]
    eye = jnp.where(lax.broadcasted_iota(jnp.int32, (qb, qb), 0) == lax.broadcasted_iota(jnp.int32, (qb, qb), 1),
                    1.0, 0.0).astype(BF16)
    ones = jnp.ones((lb, HEAD_DIM), BF16)
    stacks = [(h0 // GROUP, h0) for h0 in range(0, N_HEADS, ATTN_STACK)]
    lhs = [jnp.concatenate(
        [jnp.concatenate([q[:, h * HEAD_DIM:(h + 1) * HEAD_DIM], eye], axis=1)
         for h in range(h0, h0 + ATTN_STACK)], axis=0) for _, h0 in stacks]

    def attn_block(kb, carry):
        bias = bias_ref[kb]
        krows = kb_ref[0, pl.ds(pl.multiple_of(kb * lb, lb), lb), :]
        vrows = vb_ref[0, pl.ds(pl.multiple_of(kb * lb, lb), lb), :]
        rhs = [jnp.concatenate([krows[:, g * HEAD_DIM:(g + 1) * HEAD_DIM], bias], axis=1)
               for g in range(N_KV_HEADS)]
        vext = [jnp.concatenate([vrows[:, g * HEAD_DIM:(g + 1) * HEAD_DIM], ones], axis=1)
                for g in range(N_KV_HEADS)]
        logits = [_dot_nt(lhs_c, rhs[g]) for (g, _), lhs_c in zip(stacks, lhs)]
        m_new = [jnp.maximum(m, jnp.max(s, axis=-1, keepdims=True)) for s, (m, _) in zip(logits, carry)]
        probs = [jnp.exp2(s - mn).astype(BF16) for s, mn in zip(logits, m_new)]
        pv = [_dot(p, vext[g]) for (g, _), p in zip(stacks, probs)]
        return tuple((mn, jnp.exp2(m - mn) * acc + o) for mn, (m, acc), o in zip(m_new, carry, pv))

    def attn_pair(j, carry):
        return attn_block(2 * j + 1, attn_block(2 * j, carry))

    init = tuple((jnp.full((ATTN_STACK * qb, 1), NEG_BIG, F32),
                  jnp.zeros((ATTN_STACK * qb, 2 * HEAD_DIM), F32)) for _ in stacks)
    res = lax.fori_loop(0, jnp.right_shift(nkb, 1), attn_pair, init)
    res = lax.cond(jnp.bitwise_and(nkb, 1) == 1, lambda c: attn_block(nkb - 1, c), lambda c: c, res)
    outs = []
    for _, acc in res:
        o = acc[:, :HEAD_DIM] / acc[:, HEAD_DIM:HEAD_DIM + 1]
        outs.extend(o[r * qb:(r + 1) * qb] for r in range(ATTN_STACK))
    o_a = jnp.concatenate(outs, axis=-1)

    ya = _dot((o_a * sga_ref[0].astype(F32)).astype(BF16), wa_ref[...])
    mixed = sma_ref[0].astype(F32) * ya + pb_ref[0].astype(F32)
    xo = x_ref[0] + gate_ref[0] * _dot(mixed.astype(BF16), wout_ref[...])
    ms = jnp.mean(xo * xo, axis=-1, keepdims=True)
    y_ref[0] = xo * lax.rsqrt(ms + EPS) * gfin_ref[...]


def _attend(x, gate, g_final, q, iqw, ikw, sga, sma, pb, kall, vall, ikall, wa, wout,
            *, past, length, qb, lb):
    b, t, d = x.shape
    lp = kall.shape[1]
    nkb = lp // lb
    topk = min(TOPK_MAX, length // 4)
    grid = (b, t // qb)
    blk = lambda n: pl.BlockSpec((1, qb, n), lambda i, j: (i, j, 0))
    vec = pl.BlockSpec((1, 1, d), lambda i, j: (i, 0, 0))
    keys = lambda n: pl.BlockSpec((1, lp, n), lambda i, j: (i, 0, 0))
    full = lambda a: pl.BlockSpec(a.shape, lambda i, j: (0,) * a.ndim)
    gfin = g_final.reshape(1, -1)
    kern = functools.partial(_attend_kernel, past=past, length=length, topk=topk, lb=lb)
    return pl.pallas_call(
        kern,
        name="sparse_attend_out",
        grid=grid,
        in_specs=[blk(d), vec, full(gfin), blk(ATTN_DIM),
                  pl.BlockSpec((1, IDX_HEADS, qb, IDX_DIM), lambda i, j: (i, 0, j, 0)),
                  blk(LANES), blk(ATTN_DIM), blk(d), blk(d),
                  keys(KV_DIM), keys(KV_DIM), keys(IDX_DIM), full(wa), full(wout)],
        out_specs=blk(d),
        out_shape=jax.ShapeDtypeStruct((b, t, d), F32),
        scratch_shapes=[pltpu.VMEM((nkb, lb, qb), F32), pltpu.VMEM((nkb, lb, qb), BF16)],
        compiler_params=pltpu.CompilerParams(
            dimension_semantics=("arbitrary", "arbitrary"), vmem_limit_bytes=VMEM_LIMIT),
    )(x, gate, gfin, q, iqw, ikw, sga, sma, pb, kall, vall, ikall, wa, wout)


def _block_sizes(t):
    tq = 512 if t % 512 == 0 else t
    tc = 256 if t % 256 == 0 else t
    qb = 128 if t % 128 == 0 else t
    return tq, tc, qb, 1024


def _split_weights(w_in, d):
    o_iq = 2 * ATTN_DIM + 2 * KV_DIM
    o_iw = o_iq + IDX_HEADS * IDX_DIM
    o_ik = o_iw + IDX_HEADS
    o_glu = o_ik + IDX_DIM
    nidx = IDX_HEADS * IDX_DIM + IDX_DIM + IDX_HEADS
    pad = (-nidx) % LANES
    wqkvg = w_in[:, :o_iq].astype(BF16)
    widx = jnp.concatenate([w_in[:, o_iq:o_iw], w_in[:, o_ik:o_glu], w_in[:, o_iw:o_ik],
                            jnp.zeros((d, pad), w_in.dtype)], axis=1).astype(BF16)
    w2 = w_in[:, o_glu:].astype(BF16)
    return wqkvg, widx, w2


def _layer(x, c, past_k, past_v, past_ik, conv_state, wts, g_final):
    w_ada_bf, b_ada, g_norm, wqkvg, widx, w2, wa, w_dw, b_dw, g_ln, b_ln, wb, wout = wts
    b, t, d = x.shape
    assert t >= HALO and t % 8 == 0
    tq, tc, qb, lb = _block_sizes(t)
    gmul, shift, gate = (m.reshape(b, 1, d) for m in _modulation(c, w_ada_bf, b_ada, g_norm))
    q, k, v, kb, vb, ik, ikb, iqw, ikw, sga = _qkv(x, gmul, shift, wqkvg, widx, tq)
    pb, sma, new_conv = _conv(x, gmul, shift, w2, conv_state, w_dw, b_dw, g_ln, b_ln, wb, tc)
    past = 0 if past_k is None else past_k.shape[1]
    length = past + t
    if past:
        kb = jnp.concatenate([past_k, kb], axis=1)
        vb = jnp.concatenate([past_v, vb], axis=1)
        ikb = jnp.concatenate([past_ik, ikb], axis=1)
    pad = (-length) % lb
    if pad:
        kb, vb, ikb = (jnp.pad(a, ((0, 0), (0, pad), (0, 0))) for a in (kb, vb, ikb))
    y = _attend(x, gate, g_final, q, iqw, ikw, sga, sma, pb, kb, vb, ikb, wa, wout,
                past=past, length=length, qb=qb, lb=lb)
    return y, k, v, ik, new_conv


def kernel(x_prompt, x_sample, cache_k, cache_v, cache_idx_k, state_conv, c_prompt, c_sample,
           w_ada, b_ada, g_norm, w_in, w_a, w_dw, b_dw, g_ln, b_ln, w_b, w_out, g_final):
    depth = w_in.shape[0]
    assert depth == 1, "single-layer trunk"
    d = x_prompt.shape[-1]
    bp, tp = x_prompt.shape[:2]
    bs, ts = x_sample.shape[:2]
    wqkvg, widx, w2 = _split_weights(w_in[0], d)
    wts = (w_ada[0].astype(BF16), b_ada[0], g_norm[0], wqkvg, widx, w2, w_a[0].astype(BF16),
           w_dw[0], b_dw[0], g_ln[0], b_ln[0], w_b[0].astype(BF16), w_out[0].astype(BF16))
    zero_conv = jnp.zeros((bp, HALO, d), x_prompt.dtype)
    yp, kp, vp, ikp, cp = _layer(x_prompt, c_prompt, None, None, None, zero_conv, wts, g_final)
    pk = cache_k[0].reshape(bs, -1, KV_DIM).astype(BF16)
    pv = cache_v[0].reshape(bs, -1, KV_DIM).astype(BF16)
    pik = cache_idx_k[0].astype(BF16)
    ys, k_s, v_s, ik_s, c_s = _layer(x_sample, c_sample, pk, pv, pik, state_conv[0], wts, g_final)
    heads = lambda a: a.reshape(1, a.shape[0], a.shape[1], N_KV_HEADS, HEAD_DIM)
    return (yp, ys, heads(kp), heads(vp), ikp[None], cp[None],
            heads(k_s), heads(v_s), ik_s[None], c_s[None])
```

```python
import functools

import jax
import jax.numpy as jnp
from jax import lax
from jax.experimental import pallas as pl
from jax.experimental.pallas import tpu as pltpu

CHUNK = 64
CHUNK_SHIFT = CHUNK.bit_length() - 1
assert CHUNK == 1 << CHUNK_SHIFT
N_HEADS = 8
HEAD_DIM = 128
N_KV_HEADS = 2
GROUP = N_HEADS // N_KV_HEADS
ATTN_DIM = N_HEADS * HEAD_DIM
KV_DIM = N_KV_HEADS * HEAD_DIM
IDX_HEADS = 8
IDX_DIM = 64
TOPK_MAX = 256
CONV_WIDTH = 31
HALO = CONV_WIDTH - 1
EPS = 1e-6

LANES = 128
SUBLANES = 8
HALO_PAD = 32
NEG_BIG = -1e30
FLT_BIG = 3e38
LOG2E = 1.4426950408889634
VMEM_LIMIT = 56 * 1024 * 1024
MAX_BISECT = 40
BISECT_UNROLL = 2
assert MAX_BISECT % BISECT_UNROLL == 0
ATTN_STACK = 2
assert GROUP % ATTN_STACK == 0

BF16 = jnp.bfloat16
F32 = jnp.float32


def _silu(x):
    return x * jax.nn.sigmoid(x)


def _dot(a, b):
    return jnp.dot(a, b, preferred_element_type=F32)


def _dot_nt(a, b):
    return lax.dot_general(a, b, (((1,), (1,)), ((), ())), preferred_element_type=F32)


def _mod_kernel(c_ref, w_ref, b_ref, g_ref, gmul_ref, shift_ref, gate_ref):
    d = c_ref.shape[1]
    mod = _dot(_silu(c_ref[...]).astype(BF16), w_ref[...]) + b_ref[...]
    shift_ref[...] = mod[:, :d]
    gmul_ref[...] = g_ref[...] * (1.0 + mod[:, d:2 * d])
    gate_ref[...] = mod[:, 2 * d:]


def _modulation(c, w_ada_bf, b_ada, g_norm):
    b, d = c.shape
    out = jax.ShapeDtypeStruct((b, d), F32)
    return pl.pallas_call(
        _mod_kernel,
        name="modulation",
        out_shape=(out, out, out),
        compiler_params=pltpu.CompilerParams(vmem_limit_bytes=VMEM_LIMIT),
    )(c, w_ada_bf, b_ada.reshape(1, -1), g_norm.reshape(1, -1))


def _normed_input(x_ref, gmul_ref, shift_ref):
    x = x_ref[0]
    ms = jnp.mean(x * x, axis=-1, keepdims=True)
    return (x * lax.rsqrt(ms + EPS) * gmul_ref[0] + shift_ref[0]).astype(BF16)


def _qkv_kernel(x_ref, gmul_ref, shift_ref, wqkvg_ref, widx_ref,
                q_ref, k_ref, v_ref, kb_ref, vb_ref, ik_ref, ikb_ref, iqw_ref, ikw_ref, sga_ref):
    hb = _normed_input(x_ref, gmul_ref, shift_ref)
    z = _dot(hb, wqkvg_ref[...])
    q_ref[0] = (z[:, :ATTN_DIM] * (HEAD_DIM ** -0.5 * LOG2E)).astype(BF16)
    k = z[:, ATTN_DIM:ATTN_DIM + KV_DIM]
    v = z[:, ATTN_DIM + KV_DIM:ATTN_DIM + 2 * KV_DIM]
    k_ref[0] = k
    v_ref[0] = v
    kb_ref[0] = k.astype(BF16)
    vb_ref[0] = v.astype(BF16)
    sga_ref[0] = _silu(z[:, ATTN_DIM + 2 * KV_DIM:]).astype(BF16)
    zi = _dot(hb, widx_ref[...])
    nq = IDX_HEADS * IDX_DIM
    ik = zi[:, nq:nq + IDX_DIM]
    ik_ref[0] = ik
    ikb_ref[0] = ik.astype(BF16)
    iw = zi[:, nq + IDX_DIM:nq + IDX_DIM + IDX_HEADS]
    ikw_ref[0] = zi[:, nq:nq + LANES]
    wscale = (IDX_HEADS ** -0.5) * (IDX_DIM ** -0.5)
    for h in range(IDX_HEADS):
        iqw_ref[0, h] = (zi[:, h * IDX_DIM:(h + 1) * IDX_DIM] * (iw[:, h:h + 1] * wscale)).astype(BF16)


def _qkv(x, gmul, shift, wqkvg, widx, tb):
    b, t, d = x.shape
    grid = (b, t // tb)
    tok = lambda n, dt: jax.ShapeDtypeStruct((b, t, n), dt)
    blk = lambda n: pl.BlockSpec((1, tb, n), lambda i, j: (i, j, 0))
    vec = pl.BlockSpec((1, 1, d), lambda i, j: (i, 0, 0))
    full = lambda a: pl.BlockSpec(a.shape, lambda i, j: (0,) * a.ndim)
    return pl.pallas_call(
        _qkv_kernel,
        name="qkv_indexer_proj",
        grid=grid,
        in_specs=[blk(d), vec, vec, full(wqkvg), full(widx)],
        out_specs=(blk(ATTN_DIM), blk(KV_DIM), blk(KV_DIM), blk(KV_DIM), blk(KV_DIM),
                   blk(IDX_DIM), blk(IDX_DIM),
                   pl.BlockSpec((1, IDX_HEADS, tb, IDX_DIM), lambda i, j: (i, 0, j, 0)),
                   blk(LANES), blk(ATTN_DIM)),
        out_shape=(tok(ATTN_DIM, BF16), tok(KV_DIM, F32), tok(KV_DIM, F32), tok(KV_DIM, BF16),
                   tok(KV_DIM, BF16), tok(IDX_DIM, F32), tok(IDX_DIM, BF16),
                   jax.ShapeDtypeStruct((b, IDX_HEADS, t, IDX_DIM), BF16),
                   tok(LANES, F32), tok(ATTN_DIM, BF16)),
        compiler_params=pltpu.CompilerParams(
            dimension_semantics=("arbitrary", "arbitrary"), vmem_limit_bytes=VMEM_LIMIT),
    )(x, gmul, shift, wqkvg, widx)


def _conv_kernel(x_ref, gmul_ref, shift_ref, w2_ref, cs_ref, wdw_ref, bdw_ref, gln_ref, bln_ref, wb_ref,
                 pb_ref, sma_ref, cout_ref, ubuf, shbuf):
    tb = x_ref.shape[1]
    d = x_ref.shape[2]
    lead = HALO_PAD - HALO

    @pl.when(pl.program_id(1) == 0)
    def _():
        ubuf[0:HALO_PAD, :] = jnp.zeros((HALO_PAD, d), F32)
        ubuf[lead:HALO_PAD, :] = cs_ref[0]

    hb = _normed_input(x_ref, gmul_ref, shift_ref)
    z = _dot(hb, w2_ref[...])
    u = z[:, :d] * jax.nn.sigmoid(z[:, d:2 * d])
    ubuf[HALO_PAD:HALO_PAD + tb, :] = u
    dw = jnp.zeros((tb, d), F32) + bdw_ref[...]
    for r in range(SUBLANES):
        taps = [j for j in range(CONV_WIDTH) if (lead + j) % SUBLANES == r]
        span = lead + taps[-1] - r + tb
        src = ubuf
        if r:
            shbuf[0:span, :] = ubuf[r:r + span, :]
            src = shbuf
        for j in taps:
            off = lead + j - r
            dw = dw + src[off:off + tb, :] * wdw_ref[j:j + 1, :]
    tail = ubuf[tb + lead:tb + HALO_PAD, :]
    cout_ref[0] = tail
    ubuf[lead:HALO_PAD, :] = tail
    mu = jnp.mean(dw, axis=-1, keepdims=True)
    cen = dw - mu
    var = jnp.mean(cen * cen, axis=-1, keepdims=True)
    n = cen * lax.rsqrt(var + EPS) * gln_ref[...] + bln_ref[...]
    yb = _dot((_silu(n) * _silu(z[:, 2 * d:3 * d])).astype(BF16), wb_ref[...])
    sma_ref[0] = jax.nn.sigmoid(z[:, 3 * d:4 * d]).astype(BF16)
    pb_ref[0] = (jax.nn.sigmoid(z[:, 4 * d:5 * d]) * yb).astype(BF16)


def _conv(x, gmul, shift, w2, conv_state, w_dw, b_dw, g_ln, b_ln, wb, tb):
    b, t, d = x.shape
    grid = (b, t // tb)
    blk = pl.BlockSpec((1, tb, d), lambda i, j: (i, j, 0))
    vec = pl.BlockSpec((1, 1, d), lambda i, j: (i, 0, 0))
    halo = pl.BlockSpec((1, HALO, d), lambda i, j: (i, 0, 0))
    full = lambda a: pl.BlockSpec(a.shape, lambda i, j: (0,) * a.ndim)
    row = lambda a: a.reshape(1, -1)
    args = (x, gmul, shift, w2, conv_state, w_dw, row(b_dw), row(g_ln), row(b_ln), wb)
    return pl.pallas_call(
        _conv_kernel,
        name="conv_branch",
        grid=grid,
        in_specs=[blk, vec, vec, full(w2), halo] + [full(a) for a in args[5:]],
        out_specs=(blk, blk, halo),
        out_shape=(jax.ShapeDtypeStruct((b, t, d), BF16), jax.ShapeDtypeStruct((b, t, d), BF16),
                   jax.ShapeDtypeStruct((b, HALO, d), F32)),
        scratch_shapes=[pltpu.VMEM((tb + HALO_PAD, d), F32), pltpu.VMEM((tb + HALO_PAD, d), F32)],
        compiler_params=pltpu.CompilerParams(
            dimension_semantics=("arbitrary", "arbitrary"), vmem_limit_bytes=VMEM_LIMIT),
    )(*args)


def _attend_kernel(x_ref, gate_ref, gfin_ref, q_ref, iqw_ref, ikw_ref, sga_ref, sma_ref, pb_ref,
                   kb_ref, vb_ref, ikb_ref, wa_ref, wout_ref, y_ref, st_ref, bias_ref,
                   *, past, length, topk, lb):
    qb = x_ref.shape[1]
    i = pl.program_id(1)
    qpos0 = past + i * qb
    kf = jnp.float32(topk)
    row_chunk = min(lb, LANES)

    def visible(qpos):
        return jnp.minimum(length, (jnp.right_shift(qpos, CHUNK_SHIFT) + 1) * CHUNK)

    nfull = lax.div(visible(qpos0), jnp.int32(lb))
    nkb = lax.div(visible(qpos0 + qb - 1) + lb - 1, jnp.int32(lb))
    nadm = visible(qpos0 + lax.broadcasted_iota(jnp.int32, (1, qb), 1)).astype(F32)

    slab = ikw_ref[0]
    if qb < LANES:
        slab = jnp.concatenate([slab, jnp.zeros((LANES - qb, LANES), F32)], axis=0)
    iw_t = jnp.transpose(slab)[IDX_DIM:IDX_DIM + IDX_HEADS, :qb]
    lo_b = [jnp.where(iw_t[h:h + 1] >= 0.0, 0.0, -jnp.inf) for h in range(IDX_HEADS)]
    hi_b = [jnp.where(iw_t[h:h + 1] >= 0.0, jnp.inf, 0.0) for h in range(IDX_HEADS)]
    iq_all = iqw_ref[0].reshape(IDX_HEADS * qb, IDX_DIM)

    def score(kb):
        ikblk = ikb_ref[0, pl.ds(pl.multiple_of(kb * lb, lb), lb), :]
        dots = _dot_nt(ikblk, iq_all)
        acc = jnp.minimum(jnp.maximum(dots[:, :qb], lo_b[0]), hi_b[0])
        for h in range(1, IDX_HEADS):
            acc = acc + jnp.minimum(jnp.maximum(dots[:, h * qb:(h + 1) * qb], lo_b[h]), hi_b[h])
        return acc

    def rows(x, red):
        return red(x.reshape(x.shape[0] // SUBLANES, SUBLANES, qb), axis=0)

    def whole_block(kb, carry):
        rmin, rmax = carry
        acc = score(kb)
        st_ref[kb] = acc
        return jnp.minimum(rmin, rows(acc, jnp.min)), jnp.maximum(rmax, rows(acc, jnp.max))

    def edge_block(kb, carry):
        rmin, rmax = carry
        acc = score(kb)
        kpos = kb * lb + lax.broadcasted_iota(jnp.int32, (lb, qb), 0)
        adm = kpos.astype(F32) < nadm
        st_ref[kb] = jnp.where(adm, acc, -jnp.inf)
        return (jnp.minimum(rmin, rows(jnp.where(adm, acc, jnp.inf), jnp.min)),
                jnp.maximum(rmax, rows(jnp.where(adm, acc, -jnp.inf), jnp.max)))

    ext = (jnp.full((SUBLANES, qb), jnp.inf, F32), jnp.full((SUBLANES, qb), -jnp.inf, F32))
    ext = lax.fori_loop(0, jnp.right_shift(nfull, 1),
                        lambda j, c: whole_block(2 * j + 1, whole_block(2 * j, c)), ext)
    ext = lax.cond(jnp.bitwise_and(nfull, 1) == 1, lambda c: whole_block(nfull - 1, c), lambda c: c, ext)
    rmin, rmax = lax.fori_loop(nfull, nkb, edge_block, ext)
    rmin = jnp.min(rmin, axis=0, keepdims=True)
    rmax = jnp.max(rmax, axis=0, keepdims=True)

    def fold(pick, red, op, start):
        def body(kb, acc):
            for c in range(0, lb, row_chunk):
                acc = op(acc, rows(pick(st_ref[kb, c:c + row_chunk, :]), red))
            return acc
        return lax.fori_loop(0, nkb, body, jnp.full((SUBLANES, qb), start, F32))

    def count(pred):
        return jnp.sum(fold(lambda s: jnp.where(pred(s), 1.0, 0.0), jnp.sum, jnp.add, 0.0),
                       axis=0, keepdims=True)

    def lowest(pick):
        return jnp.min(fold(pick, jnp.min, jnp.minimum, jnp.inf), axis=0, keepdims=True)

    def highest(pick):
        return jnp.max(fold(pick, jnp.max, jnp.maximum, -jnp.inf), axis=0, keepdims=True)

    def any_query(flag):
        return jnp.max(jnp.where(flag, 1.0, 0.0)) > 0.0

    cge0 = count(lambda s: s >= 0.0)
    cgt0 = count(lambda s: s > 0.0)
    take_all = nadm <= kf
    tie0 = jnp.logical_not(take_all) & (cgt0 < kf) & (cge0 >= kf)
    pos = cgt0 >= kf
    lo = jnp.where(pos, 0.0, rmin)
    hi = jnp.where(pos, rmax, 0.0)
    done = take_all | tie0 | (jnp.where(pos, cge0, nadm) == kf)

    def bisect_cond(st):
        it, lo, hi, open_ = st
        return (it < MAX_BISECT) & (jnp.max(open_) > 0.0)

    def bisect_step(lo, hi, open_):
        mid = lo + (hi - lo) * 0.5
        c = count(lambda s: s >= mid)
        up = c >= kf
        lo = jnp.where(up, mid, lo)
        hi = jnp.where(up, hi, mid)
        settle = c == kf
        return lo, jnp.where(settle, lo, hi), jnp.where(settle, 0.0, open_)

    def bisect_body(st):
        it, lo, hi, open_ = st
        for _ in range(BISECT_UNROLL):
            lo, hi, open_ = bisect_step(lo, hi, open_)
        return it + BISECT_UNROLL, lo, hi, open_

    _, lo, hi, open_ = lax.while_loop(bisect_cond, bisect_body,
                                      (jnp.int32(0), lo, jnp.where(done, lo, hi), jnp.where(done, 0.0, 1.0)))
    done = open_ <= 0.0
    thr = jnp.where(take_all, -FLT_BIG, jnp.where(tie0, 0.0, lo))
    ties = jnp.where(tie0, kf - cgt0, jnp.inf)

    def snap_threshold():
        def snap_cond(st):
            a, b = st
            return any_query(a < b)

        def snap_body(st):
            a, b = st
            mid = a + (b - a) * 0.5
            mid = jnp.where(mid < b, mid, a)
            more = count(lambda s: s > mid) >= kf
            above = lowest(lambda s: jnp.where(s > mid, s, jnp.inf))
            below = highest(lambda s: jnp.where(s <= mid, s, -jnp.inf))
            return jnp.where(more & (a < b), above, a), jnp.where(more | (a >= b), b, below)

        a0 = lowest(lambda s: jnp.where(s >= lo, s, jnp.inf))
        b0 = highest(lambda s: jnp.where(s <= hi, s, -jnp.inf))
        a, _ = lax.while_loop(snap_cond, snap_body, (jnp.where(done, thr, a0), jnp.where(done, thr, b0)))
        want = kf - count(lambda s: s > a)
        return jnp.where(done, thr, a), jnp.where(done, ties, want)

    thr, ties = lax.cond(any_query(jnp.logical_not(done)), snap_threshold, lambda: (thr, ties))

    def mask_with_ties():
        r_i = lax.broadcasted_iota(jnp.int32, (row_chunk, row_chunk), 0)
        c_i = lax.broadcasted_iota(jnp.int32, (row_chunk, row_chunk), 1)
        tri = jnp.where(c_i <= r_i, 1.0, 0.0).astype(BF16)

        def body(kb, seen):
            for c in range(0, lb, row_chunk):
                sc = st_ref[kb, c:c + row_chunk, :]
                eq = sc == thr
                eq_f = jnp.where(eq, 1.0, 0.0)
                before = seen + _dot(tri, eq_f.astype(BF16)) - eq_f
                keep = (sc > thr) | (eq & (before < ties))
                bias_ref[kb, c:c + row_chunk, :] = jnp.where(keep, 0.0, NEG_BIG).astype(BF16)
                seen = seen + jnp.sum(eq_f, axis=0, keepdims=True)
            return seen

        lax.fori_loop(0, nkb, body, jnp.zeros((1, qb), F32))

    def mask_plain():
        def body(kb, carry):
            bias_ref[kb] = jnp.where(st_ref[kb] >= thr, 0.0, NEG_BIG).astype(BF16)
            return carry

        lax.fori_loop(0, nkb, body, jnp.int32(0))

    lax.cond(any_query(ties < jnp.inf), mask_with_ties, mask_plain)

    q = q_ref[0]
    qs = min(qb, LANES)
    eye = jnp.where(lax.broadcasted_iota(jnp.int32, (qs, qs), 0) == lax.broadcasted_iota(jnp.int32, (qs, qs), 1),
                    1.0, 0.0).astype(BF16)
    ones = jnp.ones((lb, HEAD_DIM), BF16)
    stacks = [(h0 // GROUP, h0) for h0 in range(0, N_HEADS, ATTN_STACK)]

    def sweep(u):
        lhs = [jnp.concatenate(
            [jnp.concatenate([q[u * qs:(u + 1) * qs, h * HEAD_DIM:(h + 1) * HEAD_DIM], eye], axis=1)
             for h in range(h0, h0 + ATTN_STACK)], axis=0) for _, h0 in stacks]

        def attn_block(kb, carry):
            bias = bias_ref[kb, :, u * qs:(u + 1) * qs]
            krows = kb_ref[0, pl.ds(pl.multiple_of(kb * lb, lb), lb), :]
            vrows = vb_ref[0, pl.ds(pl.multiple_of(kb * lb, lb), lb), :]
            rhs = [jnp.concatenate([krows[:, g * HEAD_DIM:(g + 1) * HEAD_DIM], bias], axis=1)
                   for g in range(N_KV_HEADS)]
            vext = [jnp.concatenate([vrows[:, g * HEAD_DIM:(g + 1) * HEAD_DIM], ones], axis=1)
                    for g in range(N_KV_HEADS)]
            logits = [_dot_nt(lhs_c, rhs[g]) for (g, _), lhs_c in zip(stacks, lhs)]
            m_new = [jnp.maximum(m, jnp.max(s, axis=-1, keepdims=True)) for s, (m, _) in zip(logits, carry)]
            probs = [jnp.exp2(s - mn).astype(BF16) for s, mn in zip(logits, m_new)]
            pv = [_dot(p, vext[g]) for (g, _), p in zip(stacks, probs)]
            return tuple((mn, jnp.exp2(m - mn) * acc + o) for mn, (m, acc), o in zip(m_new, carry, pv))

        def attn_pair(j, carry):
            return attn_block(2 * j + 1, attn_block(2 * j, carry))

        init = tuple((jnp.full((ATTN_STACK * qs, 1), NEG_BIG, F32),
                      jnp.zeros((ATTN_STACK * qs, 2 * HEAD_DIM), F32)) for _ in stacks)
        res = lax.fori_loop(0, jnp.right_shift(nkb, 1), attn_pair, init)
        res = lax.cond(jnp.bitwise_and(nkb, 1) == 1, lambda c: attn_block(nkb - 1, c), lambda c: c, res)
        outs = []
        for _, acc in res:
            o = acc[:, :HEAD_DIM] / acc[:, HEAD_DIM:HEAD_DIM + 1]
            outs.extend(o[r * qs:(r + 1) * qs] for r in range(ATTN_STACK))
        return jnp.concatenate(outs, axis=-1)

    o_a = jnp.concatenate([sweep(u) for u in range(qb // qs)], axis=0)

    ya = _dot((o_a * sga_ref[0].astype(F32)).astype(BF16), wa_ref[...])
    mixed = sma_ref[0].astype(F32) * ya + pb_ref[0].astype(F32)
    xo = x_ref[0] + gate_ref[0] * _dot(mixed.astype(BF16), wout_ref[...])
    ms = jnp.mean(xo * xo, axis=-1, keepdims=True)
    y_ref[0] = xo * lax.rsqrt(ms + EPS) * gfin_ref[...]


def _attend(x, gate, g_final, q, iqw, ikw, sga, sma, pb, kall, vall, ikall, wa, wout,
            *, past, length, qb, lb):
    b, t, d = x.shape
    lp = kall.shape[1]
    nkb = lp // lb
    topk = min(TOPK_MAX, length // 4)
    grid = (b, t // qb)
    blk = lambda n: pl.BlockSpec((1, qb, n), lambda i, j: (i, j, 0))
    vec = pl.BlockSpec((1, 1, d), lambda i, j: (i, 0, 0))
    keys = lambda n: pl.BlockSpec((1, lp, n), lambda i, j: (i, 0, 0), pipeline_mode=pl.Buffered(1))
    full = lambda a: pl.BlockSpec(a.shape, lambda i, j: (0,) * a.ndim, pipeline_mode=pl.Buffered(1))
    gfin = g_final.reshape(1, -1)
    kern = functools.partial(_attend_kernel, past=past, length=length, topk=topk, lb=lb)
    return pl.pallas_call(
        kern,
        name="sparse_attend_out",
        grid=grid,
        in_specs=[blk(d), vec, full(gfin), blk(ATTN_DIM),
                  pl.BlockSpec((1, IDX_HEADS, qb, IDX_DIM), lambda i, j: (i, 0, j, 0)),
                  blk(LANES), blk(ATTN_DIM), blk(d), blk(d),
                  keys(KV_DIM), keys(KV_DIM), keys(IDX_DIM), full(wa), full(wout)],
        out_specs=blk(d),
        out_shape=jax.ShapeDtypeStruct((b, t, d), F32),
        scratch_shapes=[pltpu.VMEM((nkb, lb, qb), F32), pltpu.VMEM((nkb, lb, qb), BF16)],
        compiler_params=pltpu.CompilerParams(
            dimension_semantics=("arbitrary", "arbitrary"), vmem_limit_bytes=VMEM_LIMIT),
    )(x, gate, gfin, q, iqw, ikw, sga, sma, pb, kall, vall, ikall, wa, wout)


def _block_sizes(t):
    tq = 512 if t % 512 == 0 else t
    tc = 256 if t % 256 == 0 else t
    qb = 256 if t % 256 == 0 else t
    return tq, tc, qb, 1024


def _split_weights(w_in, d):
    o_iq = 2 * ATTN_DIM + 2 * KV_DIM
    o_iw = o_iq + IDX_HEADS * IDX_DIM
    o_ik = o_iw + IDX_HEADS
    o_glu = o_ik + IDX_DIM
    nidx = IDX_HEADS * IDX_DIM + IDX_DIM + IDX_HEADS
    pad = (-nidx) % LANES
    wqkvg = w_in[:, :o_iq].astype(BF16)
    widx = jnp.concatenate([w_in[:, o_iq:o_iw], w_in[:, o_ik:o_glu], w_in[:, o_iw:o_ik],
                            jnp.zeros((d, pad), w_in.dtype)], axis=1).astype(BF16)
    w2 = w_in[:, o_glu:].astype(BF16)
    return wqkvg, widx, w2


def _layer(x, c, past_k, past_v, past_ik, conv_state, wts, g_final):
    w_ada_bf, b_ada, g_norm, wqkvg, widx, w2, wa, w_dw, b_dw, g_ln, b_ln, wb, wout = wts
    b, t, d = x.shape
    assert t >= HALO and t % 8 == 0
    tq, tc, qb, lb = _block_sizes(t)
    gmul, shift, gate = (m.reshape(b, 1, d) for m in _modulation(c, w_ada_bf, b_ada, g_norm))
    q, k, v, kb, vb, ik, ikb, iqw, ikw, sga = _qkv(x, gmul, shift, wqkvg, widx, tq)
    pb, sma, new_conv = _conv(x, gmul, shift, w2, conv_state, w_dw, b_dw, g_ln, b_ln, wb, tc)
    past = 0 if past_k is None else past_k.shape[1]
    length = past + t
    if past:
        kb = jnp.concatenate([past_k, kb], axis=1)
        vb = jnp.concatenate([past_v, vb], axis=1)
        ikb = jnp.concatenate([past_ik, ikb], axis=1)
    pad = (-length) % lb
    if pad:
        kb, vb, ikb = (jnp.pad(a, ((0, 0), (0, pad), (0, 0))) for a in (kb, vb, ikb))
    y = _attend(x, gate, g_final, q, iqw, ikw, sga, sma, pb, kb, vb, ikb, wa, wout,
                past=past, length=length, qb=qb, lb=lb)
    return y, k, v, ik, new_conv


def kernel(x_prompt, x_sample, cache_k, cache_v, cache_idx_k, state_conv, c_prompt, c_sample,
           w_ada, b_ada, g_norm, w_in, w_a, w_dw, b_dw, g_ln, b_ln, w_b, w_out, g_final):
    depth = w_in.shape[0]
    assert depth == 1, "single-layer trunk"
    d = x_prompt.shape[-1]
    bp, tp = x_prompt.shape[:2]
    bs, ts = x_sample.shape[:2]
    wqkvg, widx, w2 = _split_weights(w_in[0], d)
    wts = (w_ada[0].astype(BF16), b_ada[0], g_norm[0], wqkvg, widx, w2, w_a[0].astype(BF16),
           w_dw[0], b_dw[0], g_ln[0], b_ln[0], w_b[0].astype(BF16), w_out[0].astype(BF16))
    zero_conv = jnp.zeros((bp, HALO, d), x_prompt.dtype)
    yp, kp, vp, ikp, cp = _layer(x_prompt, c_prompt, None, None, None, zero_conv, wts, g_final)
    pk = cache_k[0].reshape(bs, -1, KV_DIM).astype(BF16)
    pv = cache_v[0].reshape(bs, -1, KV_DIM).astype(BF16)
    pik = cache_idx_k[0].astype(BF16)
    ys, k_s, v_s, ik_s, c_s = _layer(x_sample, c_sample, pk, pv, pik, state_conv[0], wts, g_final)
    heads = lambda a: a.reshape(1, a.shape[0], a.shape[1], N_KV_HEADS, HEAD_DIM)
    return (yp, ys, heads(kp), heads(vp), ikp[None], cp[None],
            heads(k_s), heads(v_s), ik_s[None], c_s[None])
```

```python
import functools

import jax
import jax.numpy as jnp
from jax import lax
from jax.experimental import pallas as pl
from jax.experimental.pallas import tpu as pltpu

CHUNK = 64
CHUNK_SHIFT = CHUNK.bit_length() - 1
assert CHUNK == 1 << CHUNK_SHIFT
N_HEADS = 8
HEAD_DIM = 128
N_KV_HEADS = 2
GROUP = N_HEADS // N_KV_HEADS
ATTN_DIM = N_HEADS * HEAD_DIM
KV_DIM = N_KV_HEADS * HEAD_DIM
IDX_HEADS = 8
IDX_DIM = 64
TOPK_MAX = 256
CONV_WIDTH = 31
HALO = CONV_WIDTH - 1
EPS = 1e-6

LANES = 128
SUBLANES = 8
HALO_PAD = 32
NEG_BIG = -1e30
FLT_BIG = 3e38
LOG2E = 1.4426950408889634
VMEM_LIMIT = 56 * 1024 * 1024
MAX_BISECT = 40
BISECT_UNROLL = 2
assert MAX_BISECT % BISECT_UNROLL == 0
ATTN_STACK = 2
assert GROUP % ATTN_STACK == 0

BF16 = jnp.bfloat16
F32 = jnp.float32


def _silu(x):
    return x * jax.nn.sigmoid(x)


def _dot(a, b):
    return jnp.dot(a, b, preferred_element_type=F32)


def _dot_nt(a, b):
    return lax.dot_general(a, b, (((1,), (1,)), ((), ())), preferred_element_type=F32)


def _mod_kernel(c_ref, w_ref, b_ref, g_ref, gmul_ref, shift_ref, gate_ref):
    d = c_ref.shape[1]
    mod = _dot(_silu(c_ref[...]).astype(BF16), w_ref[...]) + b_ref[...]
    shift_ref[...] = mod[:, :d]
    gmul_ref[...] = g_ref[...] * (1.0 + mod[:, d:2 * d])
    gate_ref[...] = mod[:, 2 * d:]


def _modulation(c, w_ada_bf, b_ada, g_norm):
    b, d = c.shape
    out = jax.ShapeDtypeStruct((b, d), F32)
    return pl.pallas_call(
        _mod_kernel,
        name="modulation",
        out_shape=(out, out, out),
        compiler_params=pltpu.CompilerParams(vmem_limit_bytes=VMEM_LIMIT),
    )(c, w_ada_bf, b_ada.reshape(1, -1), g_norm.reshape(1, -1))


def _normed_input(x_ref, gmul_ref, shift_ref):
    x = x_ref[0]
    ms = jnp.mean(x * x, axis=-1, keepdims=True)
    return (x * lax.rsqrt(ms + EPS) * gmul_ref[0] + shift_ref[0]).astype(BF16)


def _qkv_kernel(x_ref, gmul_ref, shift_ref, wqkvg_ref, widx_ref,
                q_ref, k_ref, v_ref, kb_ref, vb_ref, ik_ref, ikb_ref, iqw_ref, ikw_ref, sga_ref):
    hb = _normed_input(x_ref, gmul_ref, shift_ref)
    z = _dot(hb, wqkvg_ref[...])
    q_ref[0] = (z[:, :ATTN_DIM] * (HEAD_DIM ** -0.5 * LOG2E)).astype(BF16)
    k = z[:, ATTN_DIM:ATTN_DIM + KV_DIM]
    v = z[:, ATTN_DIM + KV_DIM:ATTN_DIM + 2 * KV_DIM]
    for g in range(N_KV_HEADS):
        k_ref[0, :, g, :] = k[:, g * HEAD_DIM:(g + 1) * HEAD_DIM]
        v_ref[0, :, g, :] = v[:, g * HEAD_DIM:(g + 1) * HEAD_DIM]
    kb_ref[0] = k.astype(BF16)
    vb_ref[0] = v.astype(BF16)
    sga_ref[0] = _silu(z[:, ATTN_DIM + 2 * KV_DIM:]).astype(BF16)
    zi = _dot(hb, widx_ref[...])
    nq = IDX_HEADS * IDX_DIM
    ik = zi[:, nq:nq + IDX_DIM]
    ik_ref[0] = ik
    ikb_ref[0] = ik.astype(BF16)
    iw = zi[:, nq + IDX_DIM:nq + IDX_DIM + IDX_HEADS]
    ikw_ref[0] = zi[:, nq:nq + LANES]
    wscale = (IDX_HEADS ** -0.5) * (IDX_DIM ** -0.5)
    for h in range(IDX_HEADS):
        iqw_ref[0, h] = (zi[:, h * IDX_DIM:(h + 1) * IDX_DIM] * (iw[:, h:h + 1] * wscale)).astype(BF16)


def _qkv(x, gmul, shift, wqkvg, widx, tb):
    b, t, d = x.shape
    grid = (b, t // tb)
    tok = lambda n, dt: jax.ShapeDtypeStruct((b, t, n), dt)
    blk = lambda n: pl.BlockSpec((1, tb, n), lambda i, j: (i, j, 0))
    vec = pl.BlockSpec((1, 1, d), lambda i, j: (i, 0, 0))
    full = lambda a: pl.BlockSpec(a.shape, lambda i, j: (0,) * a.ndim)
    kvblk = pl.BlockSpec((1, tb, N_KV_HEADS, HEAD_DIM), lambda i, j: (i, j, 0, 0))
    kvout = jax.ShapeDtypeStruct((b, t, N_KV_HEADS, HEAD_DIM), F32)
    return pl.pallas_call(
        _qkv_kernel,
        name="qkv_indexer_proj",
        grid=grid,
        in_specs=[blk(d), vec, vec, full(wqkvg), full(widx)],
        out_specs=(blk(ATTN_DIM), kvblk, kvblk, blk(KV_DIM), blk(KV_DIM),
                   blk(IDX_DIM), blk(IDX_DIM),
                   pl.BlockSpec((1, IDX_HEADS, tb, IDX_DIM), lambda i, j: (i, 0, j, 0)),
                   blk(LANES), blk(ATTN_DIM)),
        out_shape=(tok(ATTN_DIM, BF16), kvout, kvout, tok(KV_DIM, BF16),
                   tok(KV_DIM, BF16), tok(IDX_DIM, F32), tok(IDX_DIM, BF16),
                   jax.ShapeDtypeStruct((b, IDX_HEADS, t, IDX_DIM), BF16),
                   tok(LANES, F32), tok(ATTN_DIM, BF16)),
        compiler_params=pltpu.CompilerParams(
            dimension_semantics=("arbitrary", "arbitrary"), vmem_limit_bytes=VMEM_LIMIT),
    )(x, gmul, shift, wqkvg, widx)


def _conv_kernel(x_ref, gmul_ref, shift_ref, w2_ref, cs_ref, wdw_ref, bdw_ref, gln_ref, bln_ref, wb_ref,
                 pb_ref, sma_ref, cout_ref, ubuf, shbuf):
    tb = x_ref.shape[1]
    d = x_ref.shape[2]
    lead = HALO_PAD - HALO

    @pl.when(pl.program_id(1) == 0)
    def _():
        ubuf[0:HALO_PAD, :] = jnp.zeros((HALO_PAD, d), F32)
        ubuf[lead:HALO_PAD, :] = cs_ref[0]

    hb = _normed_input(x_ref, gmul_ref, shift_ref)
    z = _dot(hb, w2_ref[...])
    u = z[:, :d] * jax.nn.sigmoid(z[:, d:2 * d])
    ubuf[HALO_PAD:HALO_PAD + tb, :] = u
    dw = jnp.zeros((tb, d), F32) + bdw_ref[...]
    for r in range(SUBLANES):
        taps = [j for j in range(CONV_WIDTH) if (lead + j) % SUBLANES == r]
        span = lead + taps[-1] - r + tb
        src = ubuf
        if r:
            shbuf[0:span, :] = ubuf[r:r + span, :]
            src = shbuf
        for j in taps:
            off = lead + j - r
            dw = dw + src[off:off + tb, :] * wdw_ref[j:j + 1, :]
    tail = ubuf[tb + lead:tb + HALO_PAD, :]
    cout_ref[0] = tail
    ubuf[lead:HALO_PAD, :] = tail
    mu = jnp.mean(dw, axis=-1, keepdims=True)
    cen = dw - mu
    var = jnp.mean(cen * cen, axis=-1, keepdims=True)
    n = cen * lax.rsqrt(var + EPS) * gln_ref[...] + bln_ref[...]
    yb = _dot((_silu(n) * _silu(z[:, 2 * d:3 * d])).astype(BF16), wb_ref[...])
    sma_ref[0] = jax.nn.sigmoid(z[:, 3 * d:4 * d]).astype(BF16)
    pb_ref[0] = (jax.nn.sigmoid(z[:, 4 * d:5 * d]) * yb).astype(BF16)


def _conv(x, gmul, shift, w2, conv_state, w_dw, b_dw, g_ln, b_ln, wb, tb):
    b, t, d = x.shape
    grid = (b, t // tb)
    blk = pl.BlockSpec((1, tb, d), lambda i, j: (i, j, 0))
    vec = pl.BlockSpec((1, 1, d), lambda i, j: (i, 0, 0))
    halo = pl.BlockSpec((1, HALO, d), lambda i, j: (i, 0, 0))
    full = lambda a: pl.BlockSpec(a.shape, lambda i, j: (0,) * a.ndim)
    row = lambda a: a.reshape(1, -1)
    args = (x, gmul, shift, w2, conv_state, w_dw, row(b_dw), row(g_ln), row(b_ln), wb)
    return pl.pallas_call(
        _conv_kernel,
        name="conv_branch",
        grid=grid,
        in_specs=[blk, vec, vec, full(w2), halo] + [full(a) for a in args[5:]],
        out_specs=(blk, blk, halo),
        out_shape=(jax.ShapeDtypeStruct((b, t, d), BF16), jax.ShapeDtypeStruct((b, t, d), BF16),
                   jax.ShapeDtypeStruct((b, HALO, d), F32)),
        scratch_shapes=[pltpu.VMEM((tb + HALO_PAD, d), F32), pltpu.VMEM((tb + HALO_PAD, d), F32)],
        compiler_params=pltpu.CompilerParams(
            dimension_semantics=("arbitrary", "arbitrary"), vmem_limit_bytes=VMEM_LIMIT),
    )(*args)


def _attend_kernel(x_ref, gate_ref, gfin_ref, q_ref, iqw_ref, ikw_ref, sga_ref, sma_ref, pb_ref,
                   kb_ref, vb_ref, ikb_ref, wa_ref, wout_ref, y_ref, st_ref, bias_ref,
                   *, past, length, topk, lb):
    qb = x_ref.shape[1]
    i = pl.program_id(1)
    qpos0 = past + i * qb
    kf = jnp.float32(topk)
    row_chunk = min(lb, LANES)

    def visible(qpos):
        return jnp.minimum(length, (jnp.right_shift(qpos, CHUNK_SHIFT) + 1) * CHUNK)

    nfull = lax.div(visible(qpos0), jnp.int32(lb))
    nkb = lax.div(visible(qpos0 + qb - 1) + lb - 1, jnp.int32(lb))
    nadm = visible(qpos0 + lax.broadcasted_iota(jnp.int32, (1, qb), 1)).astype(F32)

    slab = ikw_ref[0]
    if qb < LANES:
        slab = jnp.concatenate([slab, jnp.zeros((LANES - qb, LANES), F32)], axis=0)
    iw_t = jnp.transpose(slab)[IDX_DIM:IDX_DIM + IDX_HEADS, :qb]
    lo_b = [jnp.where(iw_t[h:h + 1] >= 0.0, 0.0, -jnp.inf) for h in range(IDX_HEADS)]
    hi_b = [jnp.where(iw_t[h:h + 1] >= 0.0, jnp.inf, 0.0) for h in range(IDX_HEADS)]
    iq_all = iqw_ref[0].reshape(IDX_HEADS * qb, IDX_DIM)

    def score(kb):
        ikblk = ikb_ref[0, pl.ds(pl.multiple_of(kb * lb, lb), lb), :]
        dots = _dot_nt(ikblk, iq_all)
        acc = jnp.minimum(jnp.maximum(dots[:, :qb], lo_b[0]), hi_b[0])
        for h in range(1, IDX_HEADS):
            acc = acc + jnp.minimum(jnp.maximum(dots[:, h * qb:(h + 1) * qb], lo_b[h]), hi_b[h])
        return acc

    def rows(x, red):
        return red(x.reshape(x.shape[0] // SUBLANES, SUBLANES, qb), axis=0)

    def whole_block(kb, carry):
        rmin, rmax = carry
        acc = score(kb)
        st_ref[kb] = acc
        return jnp.minimum(rmin, rows(acc, jnp.min)), jnp.maximum(rmax, rows(acc, jnp.max))

    def edge_block(kb, carry):
        rmin, rmax = carry
        acc = score(kb)
        kpos = kb * lb + lax.broadcasted_iota(jnp.int32, (lb, qb), 0)
        adm = kpos.astype(F32) < nadm
        st_ref[kb] = jnp.where(adm, acc, -jnp.inf)
        return (jnp.minimum(rmin, rows(jnp.where(adm, acc, jnp.inf), jnp.min)),
                jnp.maximum(rmax, rows(jnp.where(adm, acc, -jnp.inf), jnp.max)))

    ext = (jnp.full((SUBLANES, qb), jnp.inf, F32), jnp.full((SUBLANES, qb), -jnp.inf, F32))
    ext = lax.fori_loop(0, jnp.right_shift(nfull, 1),
                        lambda j, c: whole_block(2 * j + 1, whole_block(2 * j, c)), ext)
    ext = lax.cond(jnp.bitwise_and(nfull, 1) == 1, lambda c: whole_block(nfull - 1, c), lambda c: c, ext)
    rmin, rmax = lax.fori_loop(nfull, nkb, edge_block, ext)
    rmin = jnp.min(rmin, axis=0, keepdims=True)
    rmax = jnp.max(rmax, axis=0, keepdims=True)

    def fold(pick, red, op, start):
        def body(kb, acc):
            for c in range(0, lb, row_chunk):
                acc = op(acc, rows(pick(st_ref[kb, c:c + row_chunk, :]), red))
            return acc
        return lax.fori_loop(0, nkb, body, jnp.full((SUBLANES, qb), start, F32))

    def count(pred):
        return jnp.sum(fold(lambda s: jnp.where(pred(s), 1.0, 0.0), jnp.sum, jnp.add, 0.0),
                       axis=0, keepdims=True)

    def lowest(pick):
        return jnp.min(fold(pick, jnp.min, jnp.minimum, jnp.inf), axis=0, keepdims=True)

    def highest(pick):
        return jnp.max(fold(pick, jnp.max, jnp.maximum, -jnp.inf), axis=0, keepdims=True)

    def any_query(flag):
        return jnp.max(jnp.where(flag, 1.0, 0.0)) > 0.0

    cge0 = count(lambda s: s >= 0.0)
    cgt0 = count(lambda s: s > 0.0)
    take_all = nadm <= kf
    tie0 = jnp.logical_not(take_all) & (cgt0 < kf) & (cge0 >= kf)
    pos = cgt0 >= kf
    lo = jnp.where(pos, 0.0, rmin)
    hi = jnp.where(pos, rmax, 0.0)
    done = take_all | tie0 | (jnp.where(pos, cge0, nadm) == kf)

    def bisect_cond(st):
        it, lo, hi, open_ = st
        return (it < MAX_BISECT) & (jnp.max(open_) > 0.0)

    def bisect_step(lo, hi, open_):
        mid = lo + (hi - lo) * 0.5
        c = count(lambda s: s >= mid)
        up = c >= kf
        lo = jnp.where(up, mid, lo)
        hi = jnp.where(up, hi, mid)
        settle = c == kf
        return lo, jnp.where(settle, lo, hi), jnp.where(settle, 0.0, open_)

    def bisect_body(st):
        it, lo, hi, open_ = st
        for _ in range(BISECT_UNROLL):
            lo, hi, open_ = bisect_step(lo, hi, open_)
        return it + BISECT_UNROLL, lo, hi, open_

    _, lo, hi, open_ = lax.while_loop(bisect_cond, bisect_body,
                                      (jnp.int32(0), lo, jnp.where(done, lo, hi), jnp.where(done, 0.0, 1.0)))
    done = open_ <= 0.0
    thr = jnp.where(take_all, -FLT_BIG, jnp.where(tie0, 0.0, lo))
    ties = jnp.where(tie0, kf - cgt0, jnp.inf)

    def snap_threshold():
        def snap_cond(st):
            a, b = st
            return any_query(a < b)

        def snap_body(st):
            a, b = st
            mid = a + (b - a) * 0.5
            mid = jnp.where(mid < b, mid, a)
            more = count(lambda s: s > mid) >= kf
            above = lowest(lambda s: jnp.where(s > mid, s, jnp.inf))
            below = highest(lambda s: jnp.where(s <= mid, s, -jnp.inf))
            return jnp.where(more & (a < b), above, a), jnp.where(more | (a >= b), b, below)

        a0 = lowest(lambda s: jnp.where(s >= lo, s, jnp.inf))
        b0 = highest(lambda s: jnp.where(s <= hi, s, -jnp.inf))
        a, _ = lax.while_loop(snap_cond, snap_body, (jnp.where(done, thr, a0), jnp.where(done, thr, b0)))
        want = kf - count(lambda s: s > a)
        return jnp.where(done, thr, a), jnp.where(done, ties, want)

    thr, ties = lax.cond(any_query(jnp.logical_not(done)), snap_threshold, lambda: (thr, ties))

    def mask_with_ties():
        r_i = lax.broadcasted_iota(jnp.int32, (row_chunk, row_chunk), 0)
        c_i = lax.broadcasted_iota(jnp.int32, (row_chunk, row_chunk), 1)
        tri = jnp.where(c_i <= r_i, 1.0, 0.0).astype(BF16)

        def body(kb, seen):
            for c in range(0, lb, row_chunk):
                sc = st_ref[kb, c:c + row_chunk, :]
                eq = sc == thr
                eq_f = jnp.where(eq, 1.0, 0.0)
                before = seen + _dot(tri, eq_f.astype(BF16)) - eq_f
                keep = (sc > thr) | (eq & (before < ties))
                bias_ref[kb, c:c + row_chunk, :] = jnp.where(keep, 0.0, NEG_BIG).astype(BF16)
                seen = seen + jnp.sum(eq_f, axis=0, keepdims=True)
            return seen

        lax.fori_loop(0, nkb, body, jnp.zeros((1, qb), F32))

    def mask_plain():
        def body(kb, carry):
            bias_ref[kb] = jnp.where(st_ref[kb] >= thr, 0.0, NEG_BIG).astype(BF16)
            return carry

        lax.fori_loop(0, nkb, body, jnp.int32(0))

    lax.cond(any_query(ties < jnp.inf), mask_with_ties, mask_plain)

    q = q_ref[0]
    qs = min(qb, LANES)
    eye = jnp.where(lax.broadcasted_iota(jnp.int32, (qs, qs), 0) == lax.broadcasted_iota(jnp.int32, (qs, qs), 1),
                    1.0, 0.0).astype(BF16)
    ones = jnp.ones((lb, HEAD_DIM), BF16)
    stacks = [(h0 // GROUP, h0) for h0 in range(0, N_HEADS, ATTN_STACK)]

    def sweep(u):
        lhs = [jnp.concatenate(
            [jnp.concatenate([q[u * qs:(u + 1) * qs, h * HEAD_DIM:(h + 1) * HEAD_DIM], eye], axis=1)
             for h in range(h0, h0 + ATTN_STACK)], axis=0) for _, h0 in stacks]

        def attn_block(kb, carry):
            bias = bias_ref[kb, :, u * qs:(u + 1) * qs]
            krows = kb_ref[0, pl.ds(pl.multiple_of(kb * lb, lb), lb), :]
            vrows = vb_ref[0, pl.ds(pl.multiple_of(kb * lb, lb), lb), :]
            rhs = [jnp.concatenate([krows[:, g * HEAD_DIM:(g + 1) * HEAD_DIM], bias], axis=1)
                   for g in range(N_KV_HEADS)]
            vext = [jnp.concatenate([vrows[:, g * HEAD_DIM:(g + 1) * HEAD_DIM], ones], axis=1)
                    for g in range(N_KV_HEADS)]
            logits = [_dot_nt(lhs_c, rhs[g]) for (g, _), lhs_c in zip(stacks, lhs)]
            m_new = [jnp.maximum(m, jnp.max(s, axis=-1, keepdims=True)) for s, (m, _) in zip(logits, carry)]
            probs = [jnp.exp2(s - mn).astype(BF16) for s, mn in zip(logits, m_new)]
            pv = [_dot(p, vext[g]) for (g, _), p in zip(stacks, probs)]
            return tuple((mn, jnp.exp2(m - mn) * acc + o) for mn, (m, acc), o in zip(m_new, carry, pv))

        def attn_pair(j, carry):
            return attn_block(2 * j + 1, attn_block(2 * j, carry))

        init = tuple((jnp.full((ATTN_STACK * qs, 1), NEG_BIG, F32),
                      jnp.zeros((ATTN_STACK * qs, 2 * HEAD_DIM), F32)) for _ in stacks)
        res = lax.fori_loop(0, jnp.right_shift(nkb, 1), attn_pair, init)
        res = lax.cond(jnp.bitwise_and(nkb, 1) == 1, lambda c: attn_block(nkb - 1, c), lambda c: c, res)
        outs = []
        for _, acc in res:
            o = acc[:, :HEAD_DIM] / acc[:, HEAD_DIM:HEAD_DIM + 1]
            outs.extend(o[r * qs:(r + 1) * qs] for r in range(ATTN_STACK))
        return jnp.concatenate(outs, axis=-1)

    o_a = jnp.concatenate([sweep(u) for u in range(qb // qs)], axis=0)

    ya = _dot((o_a * sga_ref[0].astype(F32)).astype(BF16), wa_ref[...])
    mixed = sma_ref[0].astype(F32) * ya + pb_ref[0].astype(F32)
    xo = x_ref[0] + gate_ref[0] * _dot(mixed.astype(BF16), wout_ref[...])
    ms = jnp.mean(xo * xo, axis=-1, keepdims=True)
    y_ref[0] = xo * lax.rsqrt(ms + EPS) * gfin_ref[...]


def _attend(x, gate, g_final, q, iqw, ikw, sga, sma, pb, kall, vall, ikall, wa, wout,
            *, past, length, qb, lb):
    b, t, d = x.shape
    lp = kall.shape[1]
    nkb = lp // lb
    topk = min(TOPK_MAX, length // 4)
    grid = (b, t // qb)
    blk = lambda n: pl.BlockSpec((1, qb, n), lambda i, j: (i, j, 0))
    vec = pl.BlockSpec((1, 1, d), lambda i, j: (i, 0, 0))
    keys = lambda n: pl.BlockSpec((1, lp, n), lambda i, j: (i, 0, 0), pipeline_mode=pl.Buffered(1))
    full = lambda a: pl.BlockSpec(a.shape, lambda i, j: (0,) * a.ndim, pipeline_mode=pl.Buffered(1))
    gfin = g_final.reshape(1, -1)
    kern = functools.partial(_attend_kernel, past=past, length=length, topk=topk, lb=lb)
    return pl.pallas_call(
        kern,
        name="sparse_attend_out",
        grid=grid,
        in_specs=[blk(d), vec, full(gfin), blk(ATTN_DIM),
                  pl.BlockSpec((1, IDX_HEADS, qb, IDX_DIM), lambda i, j: (i, 0, j, 0)),
                  blk(LANES), blk(ATTN_DIM), blk(d), blk(d),
                  keys(KV_DIM), keys(KV_DIM), keys(IDX_DIM), full(wa), full(wout)],
        out_specs=blk(d),
        out_shape=jax.ShapeDtypeStruct((b, t, d), F32),
        scratch_shapes=[pltpu.VMEM((nkb, lb, qb), F32), pltpu.VMEM((nkb, lb, qb), BF16)],
        compiler_params=pltpu.CompilerParams(
            dimension_semantics=("arbitrary", "arbitrary"), vmem_limit_bytes=VMEM_LIMIT),
    )(x, gate, gfin, q, iqw, ikw, sga, sma, pb, kall, vall, ikall, wa, wout)


def _block_sizes(t):
    tq = 512 if t % 512 == 0 else t
    tc = 256 if t % 256 == 0 else t
    qb = 256 if t % 256 == 0 else t
    return tq, tc, qb, 1024


def _split_weights(w_in, d):
    o_iq = 2 * ATTN_DIM + 2 * KV_DIM
    o_iw = o_iq + IDX_HEADS * IDX_DIM
    o_ik = o_iw + IDX_HEADS
    o_glu = o_ik + IDX_DIM
    nidx = IDX_HEADS * IDX_DIM + IDX_DIM + IDX_HEADS
    pad = (-nidx) % LANES
    wqkvg = w_in[:, :o_iq].astype(BF16)
    widx = jnp.concatenate([w_in[:, o_iq:o_iw], w_in[:, o_ik:o_glu], w_in[:, o_iw:o_ik],
                            jnp.zeros((d, pad), w_in.dtype)], axis=1).astype(BF16)
    w2 = w_in[:, o_glu:].astype(BF16)
    return wqkvg, widx, w2


def _layer(x, c, past_k, past_v, past_ik, conv_state, wts, g_final):
    w_ada_bf, b_ada, g_norm, wqkvg, widx, w2, wa, w_dw, b_dw, g_ln, b_ln, wb, wout = wts
    b, t, d = x.shape
    assert t >= HALO and t % 8 == 0
    tq, tc, qb, lb = _block_sizes(t)
    gmul, shift, gate = (m.reshape(b, 1, d) for m in _modulation(c, w_ada_bf, b_ada, g_norm))
    q, k, v, kb, vb, ik, ikb, iqw, ikw, sga = _qkv(x, gmul, shift, wqkvg, widx, tq)
    pb, sma, new_conv = _conv(x, gmul, shift, w2, conv_state, w_dw, b_dw, g_ln, b_ln, wb, tc)
    past = 0 if past_k is None else past_k.shape[1]
    length = past + t
    if past:
        kb = jnp.concatenate([past_k, kb], axis=1)
        vb = jnp.concatenate([past_v, vb], axis=1)
        ikb = jnp.concatenate([past_ik, ikb], axis=1)
    pad = (-length) % lb
    if pad:
        kb, vb, ikb = (jnp.pad(a, ((0, 0), (0, pad), (0, 0))) for a in (kb, vb, ikb))
    y = _attend(x, gate, g_final, q, iqw, ikw, sga, sma, pb, kb, vb, ikb, wa, wout,
                past=past, length=length, qb=qb, lb=lb)
    return y, k, v, ik, new_conv


def kernel(x_prompt, x_sample, cache_k, cache_v, cache_idx_k, state_conv, c_prompt, c_sample,
           w_ada, b_ada, g_norm, w_in, w_a, w_dw, b_dw, g_ln, b_ln, w_b, w_out, g_final):
    depth = w_in.shape[0]
    assert depth == 1, "single-layer trunk"
    d = x_prompt.shape[-1]
    bp, tp = x_prompt.shape[:2]
    bs, ts = x_sample.shape[:2]
    wqkvg, widx, w2 = _split_weights(w_in[0], d)
    wts = (w_ada[0].astype(BF16), b_ada[0], g_norm[0], wqkvg, widx, w2, w_a[0].astype(BF16),
           w_dw[0], b_dw[0], g_ln[0], b_ln[0], w_b[0].astype(BF16), w_out[0].astype(BF16))
    zero_conv = jnp.zeros((bp, HALO, d), x_prompt.dtype)
    yp, kp, vp, ikp, cp = _layer(x_prompt, c_prompt, None, None, None, zero_conv, wts, g_final)
    pk = cache_k[0].reshape(bs, -1, KV_DIM).astype(BF16)
    pv = cache_v[0].reshape(bs, -1, KV_DIM).astype(BF16)
    pik = cache_idx_k[0].astype(BF16)
    ys, k_s, v_s, ik_s, c_s = _layer(x_sample, c_sample, pk, pv, pik, state_conv[0], wts, g_final)
    return (yp, ys, kp[None], vp[None], ikp[None], cp[None], k_s[None], v_s[None], ik_s[None], c_s[None])
```

```python
import functools

import jax
import jax.numpy as jnp
from jax import lax
from jax.experimental import pallas as pl
from jax.experimental.pallas import tpu as pltpu

CHUNK = 64
CHUNK_SHIFT = CHUNK.bit_length() - 1
assert CHUNK == 1 << CHUNK_SHIFT
N_HEADS = 8
HEAD_DIM = 128
N_KV_HEADS = 2
GROUP = N_HEADS // N_KV_HEADS
ATTN_DIM = N_HEADS * HEAD_DIM
KV_DIM = N_KV_HEADS * HEAD_DIM
IDX_HEADS = 8
IDX_DIM = 64
TOPK_MAX = 256
CONV_WIDTH = 31
HALO = CONV_WIDTH - 1
EPS = 1e-6

LANES = 128
SUBLANES = 8
HALO_PAD = 32
NEG_BIG = -1e30
FLT_BIG = 3e38
LOG2E = 1.4426950408889634
VMEM_LIMIT = 56 * 1024 * 1024
MAX_BISECT = 40
BISECT_UNROLL = 2
assert MAX_BISECT % BISECT_UNROLL == 0
ATTN_STACK = 2
assert GROUP % ATTN_STACK == 0

BF16 = jnp.bfloat16
F32 = jnp.float32


def _silu(x):
    return x * jax.nn.sigmoid(x)


def _dot(a, b):
    return jnp.dot(a, b, preferred_element_type=F32)


def _dot_nt(a, b):
    return lax.dot_general(a, b, (((1,), (1,)), ((), ())), preferred_element_type=F32)


def _mod_kernel(c_ref, w_ref, b_ref, g_ref, gmul_ref, shift_ref, gate_ref):
    d = c_ref.shape[1]
    mod = _dot(_silu(c_ref[...]).astype(BF16), w_ref[...]) + b_ref[...]
    shift_ref[...] = mod[:, :d]
    gmul_ref[...] = g_ref[...] * (1.0 + mod[:, d:2 * d])
    gate_ref[...] = mod[:, 2 * d:]


def _modulation(c, w_ada_bf, b_ada, g_norm):
    b, d = c.shape
    out = jax.ShapeDtypeStruct((b, d), F32)
    return pl.pallas_call(
        _mod_kernel,
        name="modulation",
        out_shape=(out, out, out),
        compiler_params=pltpu.CompilerParams(vmem_limit_bytes=VMEM_LIMIT),
    )(c, w_ada_bf, b_ada.reshape(1, -1), g_norm.reshape(1, -1))


def _normed_input(x_ref, gmul_ref, shift_ref):
    x = x_ref[0]
    ms = jnp.mean(x * x, axis=-1, keepdims=True)
    return (x * lax.rsqrt(ms + EPS) * gmul_ref[0] + shift_ref[0]).astype(BF16)


def _qkv_body(hb, wqkvg_ref, widx_ref,
              q_ref, k_ref, v_ref, kb_ref, vb_ref, ik_ref, ikb_ref, iqw_ref, ikw_ref, sga_ref):
    z = _dot(hb, wqkvg_ref[...])
    q_ref[0] = (z[:, :ATTN_DIM] * (HEAD_DIM ** -0.5 * LOG2E)).astype(BF16)
    k = z[:, ATTN_DIM:ATTN_DIM + KV_DIM]
    v = z[:, ATTN_DIM + KV_DIM:ATTN_DIM + 2 * KV_DIM]
    for g in range(N_KV_HEADS):
        k_ref[0, :, g, :] = k[:, g * HEAD_DIM:(g + 1) * HEAD_DIM]
        v_ref[0, :, g, :] = v[:, g * HEAD_DIM:(g + 1) * HEAD_DIM]
    kb_ref[0] = k.astype(BF16)
    vb_ref[0] = v.astype(BF16)
    sga_ref[0] = _silu(z[:, ATTN_DIM + 2 * KV_DIM:]).astype(BF16)
    zi = _dot(hb, widx_ref[...])
    nq = IDX_HEADS * IDX_DIM
    ik = zi[:, nq:nq + IDX_DIM]
    ik_ref[0] = ik
    ikb_ref[0] = ik.astype(BF16)
    iw = zi[:, nq + IDX_DIM:nq + IDX_DIM + IDX_HEADS]
    ikw_ref[0] = zi[:, nq:nq + LANES]
    wscale = (IDX_HEADS ** -0.5) * (IDX_DIM ** -0.5)
    for h in range(IDX_HEADS):
        iqw_ref[0, h] = (zi[:, h * IDX_DIM:(h + 1) * IDX_DIM] * (iw[:, h:h + 1] * wscale)).astype(BF16)


def _conv_body(hb, w2_ref, cs_ref, wdw_ref, bdw_ref, gln_ref, bln_ref, wb_ref,
               pb_ref, sma_ref, cout_ref, ubuf, shbuf):
    tb, d = hb.shape
    lead = HALO_PAD - HALO

    @pl.when(pl.program_id(1) == 0)
    def _():
        ubuf[0:HALO_PAD, :] = jnp.zeros((HALO_PAD, d), F32)
        ubuf[lead:HALO_PAD, :] = cs_ref[0]

    z = _dot(hb, w2_ref[...])
    u = z[:, :d] * jax.nn.sigmoid(z[:, d:2 * d])
    ubuf[HALO_PAD:HALO_PAD + tb, :] = u
    dw = jnp.zeros((tb, d), F32) + bdw_ref[...]
    for r in range(SUBLANES):
        taps = [j for j in range(CONV_WIDTH) if (lead + j) % SUBLANES == r]
        span = lead + taps[-1] - r + tb
        src = ubuf
        if r:
            shbuf[0:span, :] = ubuf[r:r + span, :]
            src = shbuf
        for j in taps:
            off = lead + j - r
            dw = dw + src[off:off + tb, :] * wdw_ref[j:j + 1, :]
    tail = ubuf[tb + lead:tb + HALO_PAD, :]
    cout_ref[0] = tail
    ubuf[lead:HALO_PAD, :] = tail
    mu = jnp.mean(dw, axis=-1, keepdims=True)
    cen = dw - mu
    var = jnp.mean(cen * cen, axis=-1, keepdims=True)
    n = cen * lax.rsqrt(var + EPS) * gln_ref[...] + bln_ref[...]
    yb = _dot((_silu(n) * _silu(z[:, 2 * d:3 * d])).astype(BF16), wb_ref[...])
    sma_ref[0] = jax.nn.sigmoid(z[:, 3 * d:4 * d]).astype(BF16)
    pb_ref[0] = (jax.nn.sigmoid(z[:, 4 * d:5 * d]) * yb).astype(BF16)


def _proj_kernel(x_ref, gmul_ref, shift_ref, wqkvg_ref, widx_ref, w2_ref, cs_ref, wdw_ref, bdw_ref, gln_ref,
                 bln_ref, wb_ref, q_ref, k_ref, v_ref, kb_ref, vb_ref, ik_ref, ikb_ref, iqw_ref, ikw_ref,
                 sga_ref, pb_ref, sma_ref, cout_ref, ubuf, shbuf):
    hb = _normed_input(x_ref, gmul_ref, shift_ref)
    _qkv_body(hb, wqkvg_ref, widx_ref, q_ref, k_ref, v_ref, kb_ref, vb_ref, ik_ref, ikb_ref, iqw_ref,
              ikw_ref, sga_ref)
    _conv_body(hb, w2_ref, cs_ref, wdw_ref, bdw_ref, gln_ref, bln_ref, wb_ref, pb_ref, sma_ref, cout_ref,
               ubuf, shbuf)


def _proj(x, gmul, shift, wqkvg, widx, w2, conv_state, w_dw, b_dw, g_ln, b_ln, wb, tb):
    b, t, d = x.shape
    grid = (b, t // tb)
    tok = lambda n, dt: jax.ShapeDtypeStruct((b, t, n), dt)
    blk = lambda n: pl.BlockSpec((1, tb, n), lambda i, j: (i, j, 0))
    vec = pl.BlockSpec((1, 1, d), lambda i, j: (i, 0, 0))
    halo = pl.BlockSpec((1, HALO, d), lambda i, j: (i, 0, 0))
    full = lambda a: pl.BlockSpec(a.shape, lambda i, j: (0,) * a.ndim, pipeline_mode=pl.Buffered(1))
    kvblk = pl.BlockSpec((1, tb, N_KV_HEADS, HEAD_DIM), lambda i, j: (i, j, 0, 0))
    kvout = jax.ShapeDtypeStruct((b, t, N_KV_HEADS, HEAD_DIM), F32)
    row = lambda a: a.reshape(1, -1)
    args = (x, gmul, shift, wqkvg, widx, w2, conv_state, w_dw, row(b_dw), row(g_ln), row(b_ln), wb)
    return pl.pallas_call(
        _proj_kernel,
        name="input_proj_conv",
        grid=grid,
        in_specs=[blk(d), vec, vec, full(wqkvg), full(widx), full(w2), halo] + [full(a) for a in args[7:]],
        out_specs=(blk(ATTN_DIM), kvblk, kvblk, blk(KV_DIM), blk(KV_DIM),
                   blk(IDX_DIM), blk(IDX_DIM),
                   pl.BlockSpec((1, IDX_HEADS, tb, IDX_DIM), lambda i, j: (i, 0, j, 0)),
                   blk(LANES), blk(ATTN_DIM), blk(d), blk(d), halo),
        out_shape=(tok(ATTN_DIM, BF16), kvout, kvout, tok(KV_DIM, BF16),
                   tok(KV_DIM, BF16), tok(IDX_DIM, F32), tok(IDX_DIM, BF16),
                   jax.ShapeDtypeStruct((b, IDX_HEADS, t, IDX_DIM), BF16),
                   tok(LANES, F32), tok(ATTN_DIM, BF16), tok(d, BF16), tok(d, BF16),
                   jax.ShapeDtypeStruct((b, HALO, d), F32)),
        scratch_shapes=[pltpu.VMEM((tb + HALO_PAD, d), F32), pltpu.VMEM((tb + HALO_PAD, d), F32)],
        compiler_params=pltpu.CompilerParams(
            dimension_semantics=("arbitrary", "arbitrary"), vmem_limit_bytes=VMEM_LIMIT),
    )(*args)


def _attend_kernel(x_ref, gate_ref, gfin_ref, q_ref, iqw_ref, ikw_ref, sga_ref, sma_ref, pb_ref,
                   kb_ref, vb_ref, ikb_ref, wa_ref, wout_ref, y_ref, st_ref, bias_ref,
                   *, past, length, topk, lb):
    qb = x_ref.shape[1]
    i = pl.program_id(1)
    qpos0 = past + i * qb
    kf = jnp.float32(topk)
    row_chunk = min(lb, LANES)

    def visible(qpos):
        return jnp.minimum(length, (jnp.right_shift(qpos, CHUNK_SHIFT) + 1) * CHUNK)

    nfull = lax.div(visible(qpos0), jnp.int32(lb))
    nkb = lax.div(visible(qpos0 + qb - 1) + lb - 1, jnp.int32(lb))
    nadm = visible(qpos0 + lax.broadcasted_iota(jnp.int32, (1, qb), 1)).astype(F32)

    slab = ikw_ref[0]
    if qb < LANES:
        slab = jnp.concatenate([slab, jnp.zeros((LANES - qb, LANES), F32)], axis=0)
    iw_t = jnp.transpose(slab)[IDX_DIM:IDX_DIM + IDX_HEADS, :qb]
    lo_b = [jnp.where(iw_t[h:h + 1] >= 0.0, 0.0, -jnp.inf) for h in range(IDX_HEADS)]
    hi_b = [jnp.where(iw_t[h:h + 1] >= 0.0, jnp.inf, 0.0) for h in range(IDX_HEADS)]
    iq_all = iqw_ref[0].reshape(IDX_HEADS * qb, IDX_DIM)

    def score(kb):
        ikblk = ikb_ref[0, pl.ds(pl.multiple_of(kb * lb, lb), lb), :]
        dots = _dot_nt(ikblk, iq_all)
        acc = jnp.minimum(jnp.maximum(dots[:, :qb], lo_b[0]), hi_b[0])
        for h in range(1, IDX_HEADS):
            acc = acc + jnp.minimum(jnp.maximum(dots[:, h * qb:(h + 1) * qb], lo_b[h]), hi_b[h])
        return acc

    def rows(x, red):
        return red(x.reshape(x.shape[0] // SUBLANES, SUBLANES, qb), axis=0)

    def whole_block(kb, carry):
        rmin, rmax = carry
        acc = score(kb)
        st_ref[kb] = acc
        return jnp.minimum(rmin, rows(acc, jnp.min)), jnp.maximum(rmax, rows(acc, jnp.max))

    def edge_block(kb, carry):
        rmin, rmax = carry
        acc = score(kb)
        kpos = kb * lb + lax.broadcasted_iota(jnp.int32, (lb, qb), 0)
        adm = kpos.astype(F32) < nadm
        st_ref[kb] = jnp.where(adm, acc, -jnp.inf)
        return (jnp.minimum(rmin, rows(jnp.where(adm, acc, jnp.inf), jnp.min)),
                jnp.maximum(rmax, rows(jnp.where(adm, acc, -jnp.inf), jnp.max)))

    ext = (jnp.full((SUBLANES, qb), jnp.inf, F32), jnp.full((SUBLANES, qb), -jnp.inf, F32))
    ext = lax.fori_loop(0, jnp.right_shift(nfull, 1),
                        lambda j, c: whole_block(2 * j + 1, whole_block(2 * j, c)), ext)
    ext = lax.cond(jnp.bitwise_and(nfull, 1) == 1, lambda c: whole_block(nfull - 1, c), lambda c: c, ext)
    rmin, rmax = lax.fori_loop(nfull, nkb, edge_block, ext)
    rmin = jnp.min(rmin, axis=0, keepdims=True)
    rmax = jnp.max(rmax, axis=0, keepdims=True)

    def fold(pick, red, op, start):
        def body(kb, acc):
            for c in range(0, lb, row_chunk):
                acc = op(acc, rows(pick(st_ref[kb, c:c + row_chunk, :]), red))
            return acc
        return lax.fori_loop(0, nkb, body, jnp.full((SUBLANES, qb), start, F32))

    def count(pred):
        return jnp.sum(fold(lambda s: jnp.where(pred(s), 1.0, 0.0), jnp.sum, jnp.add, 0.0),
                       axis=0, keepdims=True)

    def lowest(pick):
        return jnp.min(fold(pick, jnp.min, jnp.minimum, jnp.inf), axis=0, keepdims=True)

    def highest(pick):
        return jnp.max(fold(pick, jnp.max, jnp.maximum, -jnp.inf), axis=0, keepdims=True)

    def any_query(flag):
        return jnp.max(jnp.where(flag, 1.0, 0.0)) > 0.0

    cge0 = count(lambda s: s >= 0.0)
    cgt0 = count(lambda s: s > 0.0)
    take_all = nadm <= kf
    tie0 = jnp.logical_not(take_all) & (cgt0 < kf) & (cge0 >= kf)
    pos = cgt0 >= kf
    lo = jnp.where(pos, 0.0, rmin)
    hi = jnp.where(pos, rmax, 0.0)
    done = take_all | tie0 | (jnp.where(pos, cge0, nadm) == kf)

    def bisect_cond(st):
        it, lo, hi, open_ = st
        return (it < MAX_BISECT) & (jnp.max(open_) > 0.0)

    def bisect_step(lo, hi, open_):
        mid = lo + (hi - lo) * 0.5
        c = count(lambda s: s >= mid)
        up = c >= kf
        lo = jnp.where(up, mid, lo)
        hi = jnp.where(up, hi, mid)
        settle = c == kf
        return lo, jnp.where(settle, lo, hi), jnp.where(settle, 0.0, open_)

    def bisect_body(st):
        it, lo, hi, open_ = st
        for _ in range(BISECT_UNROLL):
            lo, hi, open_ = bisect_step(lo, hi, open_)
        return it + BISECT_UNROLL, lo, hi, open_

    _, lo, hi, open_ = lax.while_loop(bisect_cond, bisect_body,
                                      (jnp.int32(0), lo, jnp.where(done, lo, hi), jnp.where(done, 0.0, 1.0)))
    done = open_ <= 0.0
    thr = jnp.where(take_all, -FLT_BIG, jnp.where(tie0, 0.0, lo))
    ties = jnp.where(tie0, kf - cgt0, jnp.inf)

    def snap_threshold():
        def snap_cond(st):
            a, b = st
            return any_query(a < b)

        def snap_body(st):
            a, b = st
            mid = a + (b - a) * 0.5
            mid = jnp.where(mid < b, mid, a)
            more = count(lambda s: s > mid) >= kf
            above = lowest(lambda s: jnp.where(s > mid, s, jnp.inf))
            below = highest(lambda s: jnp.where(s <= mid, s, -jnp.inf))
            return jnp.where(more & (a < b), above, a), jnp.where(more | (a >= b), b, below)

        a0 = lowest(lambda s: jnp.where(s >= lo, s, jnp.inf))
        b0 = highest(lambda s: jnp.where(s <= hi, s, -jnp.inf))
        a, _ = lax.while_loop(snap_cond, snap_body, (jnp.where(done, thr, a0), jnp.where(done, thr, b0)))
        want = kf - count(lambda s: s > a)
        return jnp.where(done, thr, a), jnp.where(done, ties, want)

    thr, ties = lax.cond(any_query(jnp.logical_not(done)), snap_threshold, lambda: (thr, ties))

    def mask_with_ties():
        r_i = lax.broadcasted_iota(jnp.int32, (row_chunk, row_chunk), 0)
        c_i = lax.broadcasted_iota(jnp.int32, (row_chunk, row_chunk), 1)
        tri = jnp.where(c_i <= r_i, 1.0, 0.0).astype(BF16)

        def body(kb, seen):
            for c in range(0, lb, row_chunk):
                sc = st_ref[kb, c:c + row_chunk, :]
                eq = sc == thr
                eq_f = jnp.where(eq, 1.0, 0.0)
                before = seen + _dot(tri, eq_f.astype(BF16)) - eq_f
                keep = (sc > thr) | (eq & (before < ties))
                bias_ref[kb, c:c + row_chunk, :] = jnp.where(keep, 0.0, NEG_BIG).astype(BF16)
                seen = seen + jnp.sum(eq_f, axis=0, keepdims=True)
            return seen

        lax.fori_loop(0, nkb, body, jnp.zeros((1, qb), F32))

    def mask_plain():
        def body(kb, carry):
            bias_ref[kb] = jnp.where(st_ref[kb] >= thr, 0.0, NEG_BIG).astype(BF16)
            return carry

        lax.fori_loop(0, nkb, body, jnp.int32(0))

    lax.cond(any_query(ties < jnp.inf), mask_with_ties, mask_plain)

    q = q_ref[0]
    qs = min(qb, LANES)
    eye = jnp.where(lax.broadcasted_iota(jnp.int32, (qs, qs), 0) == lax.broadcasted_iota(jnp.int32, (qs, qs), 1),
                    1.0, 0.0).astype(BF16)
    ones = jnp.ones((lb, HEAD_DIM), BF16)
    stacks = [(h0 // GROUP, h0) for h0 in range(0, N_HEADS, ATTN_STACK)]

    def sweep(u):
        lhs = [jnp.concatenate(
            [jnp.concatenate([q[u * qs:(u + 1) * qs, h * HEAD_DIM:(h + 1) * HEAD_DIM], eye], axis=1)
             for h in range(h0, h0 + ATTN_STACK)], axis=0) for _, h0 in stacks]

        def attn_block(kb, carry):
            bias = bias_ref[kb, :, u * qs:(u + 1) * qs]
            krows = kb_ref[0, pl.ds(pl.multiple_of(kb * lb, lb), lb), :]
            vrows = vb_ref[0, pl.ds(pl.multiple_of(kb * lb, lb), lb), :]
            rhs = [jnp.concatenate([krows[:, g * HEAD_DIM:(g + 1) * HEAD_DIM], bias], axis=1)
                   for g in range(N_KV_HEADS)]
            vext = [jnp.concatenate([vrows[:, g * HEAD_DIM:(g + 1) * HEAD_DIM], ones], axis=1)
                    for g in range(N_KV_HEADS)]
            logits = [_dot_nt(lhs_c, rhs[g]) for (g, _), lhs_c in zip(stacks, lhs)]
            m_new = [jnp.maximum(m, jnp.max(s, axis=-1, keepdims=True)) for s, (m, _) in zip(logits, carry)]
            probs = [jnp.exp2(s - mn).astype(BF16) for s, mn in zip(logits, m_new)]
            pv = [_dot(p, vext[g]) for (g, _), p in zip(stacks, probs)]
            return tuple((mn, jnp.exp2(m - mn) * acc + o) for mn, (m, acc), o in zip(m_new, carry, pv))

        def attn_pair(j, carry):
            return attn_block(2 * j + 1, attn_block(2 * j, carry))

        init = tuple((jnp.full((ATTN_STACK * qs, 1), NEG_BIG, F32),
                      jnp.zeros((ATTN_STACK * qs, 2 * HEAD_DIM), F32)) for _ in stacks)
        res = lax.fori_loop(0, jnp.right_shift(nkb, 1), attn_pair, init)
        res = lax.cond(jnp.bitwise_and(nkb, 1) == 1, lambda c: attn_block(nkb - 1, c), lambda c: c, res)
        outs = []
        for _, acc in res:
            o = acc[:, :HEAD_DIM] / acc[:, HEAD_DIM:HEAD_DIM + 1]
            outs.extend(o[r * qs:(r + 1) * qs] for r in range(ATTN_STACK))
        return jnp.concatenate(outs, axis=-1)

    o_a = jnp.concatenate([sweep(u) for u in range(qb // qs)], axis=0)

    ya = _dot((o_a * sga_ref[0].astype(F32)).astype(BF16), wa_ref[...])
    mixed = sma_ref[0].astype(F32) * ya + pb_ref[0].astype(F32)
    xo = x_ref[0] + gate_ref[0] * _dot(mixed.astype(BF16), wout_ref[...])
    ms = jnp.mean(xo * xo, axis=-1, keepdims=True)
    y_ref[0] = xo * lax.rsqrt(ms + EPS) * gfin_ref[...]


def _attend(x, gate, g_final, q, iqw, ikw, sga, sma, pb, kall, vall, ikall, wa, wout,
            *, past, length, qb, lb):
    b, t, d = x.shape
    lp = kall.shape[1]
    nkb = lp // lb
    topk = min(TOPK_MAX, length // 4)
    grid = (b, t // qb)
    blk = lambda n: pl.BlockSpec((1, qb, n), lambda i, j: (i, j, 0))
    vec = pl.BlockSpec((1, 1, d), lambda i, j: (i, 0, 0))
    keys = lambda n: pl.BlockSpec((1, lp, n), lambda i, j: (i, 0, 0), pipeline_mode=pl.Buffered(1))
    full = lambda a: pl.BlockSpec(a.shape, lambda i, j: (0,) * a.ndim, pipeline_mode=pl.Buffered(1))
    gfin = g_final.reshape(1, -1)
    kern = functools.partial(_attend_kernel, past=past, length=length, topk=topk, lb=lb)
    return pl.pallas_call(
        kern,
        name="sparse_attend_out",
        grid=grid,
        in_specs=[blk(d), vec, full(gfin), blk(ATTN_DIM),
                  pl.BlockSpec((1, IDX_HEADS, qb, IDX_DIM), lambda i, j: (i, 0, j, 0)),
                  blk(LANES), blk(ATTN_DIM), blk(d), blk(d),
                  keys(KV_DIM), keys(KV_DIM), keys(IDX_DIM), full(wa), full(wout)],
        out_specs=blk(d),
        out_shape=jax.ShapeDtypeStruct((b, t, d), F32),
        scratch_shapes=[pltpu.VMEM((nkb, lb, qb), F32), pltpu.VMEM((nkb, lb, qb), BF16)],
        compiler_params=pltpu.CompilerParams(
            dimension_semantics=("arbitrary", "arbitrary"), vmem_limit_bytes=VMEM_LIMIT),
    )(x, gate, gfin, q, iqw, ikw, sga, sma, pb, kall, vall, ikall, wa, wout)


def _block_sizes(t):
    tp = 256 if t % 256 == 0 else t
    qb = 256 if t % 256 == 0 else t
    return tp, qb, 1024


def _split_weights(w_in, d):
    o_iq = 2 * ATTN_DIM + 2 * KV_DIM
    o_iw = o_iq + IDX_HEADS * IDX_DIM
    o_ik = o_iw + IDX_HEADS
    o_glu = o_ik + IDX_DIM
    nidx = IDX_HEADS * IDX_DIM + IDX_DIM + IDX_HEADS
    pad = (-nidx) % LANES
    wqkvg = w_in[:, :o_iq].astype(BF16)
    widx = jnp.concatenate([w_in[:, o_iq:o_iw], w_in[:, o_ik:o_glu], w_in[:, o_iw:o_ik],
                            jnp.zeros((d, pad), w_in.dtype)], axis=1).astype(BF16)
    w2 = w_in[:, o_glu:].astype(BF16)
    return wqkvg, widx, w2


def _layer(x, c, past_k, past_v, past_ik, conv_state, wts, g_final):
    w_ada_bf, b_ada, g_norm, wqkvg, widx, w2, wa, w_dw, b_dw, g_ln, b_ln, wb, wout = wts
    b, t, d = x.shape
    assert t >= HALO and t % 8 == 0
    tp, qb, lb = _block_sizes(t)
    gmul, shift, gate = (m.reshape(b, 1, d) for m in _modulation(c, w_ada_bf, b_ada, g_norm))
    q, k, v, kb, vb, ik, ikb, iqw, ikw, sga, pb, sma, new_conv = _proj(
        x, gmul, shift, wqkvg, widx, w2, conv_state, w_dw, b_dw, g_ln, b_ln, wb, tp)
    past = 0 if past_k is None else past_k.shape[1]
    length = past + t
    if past:
        kb = jnp.concatenate([past_k, kb], axis=1)
        vb = jnp.concatenate([past_v, vb], axis=1)
        ikb = jnp.concatenate([past_ik, ikb], axis=1)
    pad = (-length) % lb
    if pad:
        kb, vb, ikb = (jnp.pad(a, ((0, 0), (0, pad), (0, 0))) for a in (kb, vb, ikb))
    y = _attend(x, gate, g_final, q, iqw, ikw, sga, sma, pb, kb, vb, ikb, wa, wout,
                past=past, length=length, qb=qb, lb=lb)
    return y, k, v, ik, new_conv


def kernel(x_prompt, x_sample, cache_k, cache_v, cache_idx_k, state_conv, c_prompt, c_sample,
           w_ada, b_ada, g_norm, w_in, w_a, w_dw, b_dw, g_ln, b_ln, w_b, w_out, g_final):
    depth = w_in.shape[0]
    assert depth == 1, "single-layer trunk"
    d = x_prompt.shape[-1]
    bp, tp = x_prompt.shape[:2]
    bs, ts = x_sample.shape[:2]
    wqkvg, widx, w2 = _split_weights(w_in[0], d)
    wts = (w_ada[0].astype(BF16), b_ada[0], g_norm[0], wqkvg, widx, w2, w_a[0].astype(BF16),
           w_dw[0], b_dw[0], g_ln[0], b_ln[0], w_b[0].astype(BF16), w_out[0].astype(BF16))
    zero_conv = jnp.zeros((bp, HALO, d), x_prompt.dtype)
    yp, kp, vp, ikp, cp = _layer(x_prompt, c_prompt, None, None, None, zero_conv, wts, g_final)
    pk = cache_k[0].reshape(bs, -1, KV_DIM).astype(BF16)
    pv = cache_v[0].reshape(bs, -1, KV_DIM).astype(BF16)
    pik = cache_idx_k[0].astype(BF16)
    ys, k_s, v_s, ik_s, c_s = _layer(x_sample, c_sample, pk, pv, pik, state_conv[0], wts, g_final)
    return (yp, ys, kp[None], vp[None], ikp[None], cp[None], k_s[None], v_s[None], ik_s[None], c_s[None])
```
